```python
import jax, jax.numpy as jnp
from jax import lax
import numpy as np

D_MODEL = 2048
BATCH = 8
SEQ = 4096
DEPTH = 1
DEC_BATCH = 4
DEC_SEQ = 4096
PAST_LEN = 128

GRID_W = 64
HEAD_DIM = 128
NA_HEADS = 8
GQA_HEADS = 8
GQA_KV_HEADS = 2
NA_WIDTH = NA_HEADS * HEAD_DIM
GQA_WIDTH = GQA_HEADS * HEAD_DIM
KV_WIDTH = GQA_KV_HEADS * HEAD_DIM
NA_ROWS_MAX = 8
NA_COLS = 16
D_FF = 4 * D_MODEL
PLE_DIM = 256
ROPE_THETA = 10000.0
Q_BLOCK = 128
EPS = 1e-6
NEG_INF = -1e30
IN_COLS = 3 * NA_WIDTH + GQA_WIDTH + 2 * KV_WIDTH + 2 * D_MODEL

kernel_name = "hybrid_natten_gqa_encoder"


def rms_norm(x, g):
    xf = x.astype(jnp.float32)
    y = xf * lax.rsqrt(jnp.mean(xf * xf, axis=-1, keepdims=True) + EPS)
    return (y * g.astype(jnp.float32)).astype(x.dtype)


def neighbourhood_attention(q, k, v, rpb):
    B, S, H, Dh = q.shape
    rows = S // GRID_W
    kr = min(NA_ROWS_MAX, rows)
    r = jnp.arange(rows)
    c = jnp.arange(GRID_W)
    row_start = jnp.clip(r - kr // 2, 0, rows - kr)
    key_rows = row_start[:, None] + jnp.arange(kr)[None, :]
    col_start = jnp.clip(c - NA_COLS // 2, 0, GRID_W - NA_COLS)
    qg = q.reshape(B, rows, GRID_W, H, Dh)
    kg = k.reshape(B, rows, GRID_W, H, Dh)[:, key_rows]
    vg = v.reshape(B, rows, GRID_W, H, Dh)[:, key_rows]
    scale = Dh ** -0.5
    s = jnp.einsum('brqhd,brjkhd->brhqjk', qg.astype(jnp.float32) * scale,
                   kg.astype(jnp.float32))
    dr = key_rows - r[:, None]
    dc = jnp.clip(c[None, :] - c[:, None], -(NA_COLS - 1), NA_COLS - 1)
    bias = rpb[:, (dr + NA_ROWS_MAX - 1)[:, :, None, None],
               (dc + NA_COLS - 1)[None, None, :, :]]
    bias = bias.transpose(1, 0, 3, 2, 4).astype(jnp.float32)
    col_ok = (c[None, :] >= col_start[:, None]) & (c[None, :] < col_start[:, None] + NA_COLS)
    s = jnp.where(col_ok[:, None, :], s + bias[None], NEG_INF)
    p = jax.nn.softmax(s, axis=(-2, -1))
    out = jnp.einsum('brhqjk,brjkhd->brqhd', p.astype(v.dtype), vg)
    return out.reshape(B, S, H * Dh)


def rope_1d(x, pos):
    d = x.shape[-1]
    freqs = ROPE_THETA ** (-jnp.arange(0, d, 2, dtype=jnp.float32) / d)
    ang = pos.astype(jnp.float32)[:, None] * freqs[None, :]
    cos = jnp.cos(ang)[:, None, :]
    sin = jnp.sin(ang)[:, None, :]
    xf = x.astype(jnp.float32)
    x1, x2 = xf[..., : d // 2], xf[..., d // 2:]
    return jnp.concatenate([x1 * cos - x2 * sin, x2 * cos + x1 * sin], axis=-1).astype(x.dtype)


def axial_rope(x):
    S = x.shape[1]
    t = jnp.arange(S)
    half = x.shape[-1] // 2
    return jnp.concatenate([rope_1d(x[..., :half], t // GRID_W),
                            rope_1d(x[..., half:], t % GRID_W)], axis=-1)


def gqa_attention(q, k, v):
    B, S, Hq, Dh = q.shape
    Hkv = k.shape[2]
    G = Hq // Hkv
    nblk = S // Q_BLOCK
    scale = Dh ** -0.5
    qb = q.reshape(B, nblk, Q_BLOCK, Hkv, G, Dh).transpose(1, 0, 2, 3, 4, 5)
    kf = k.astype(jnp.float32)

    def attend(qblk):
        s = jnp.einsum('bqkgd,bskd->bkgqs', qblk.astype(jnp.float32) * scale, kf)
        p = jax.nn.softmax(s, axis=-1)
        return jnp.einsum('bkgqs,bskd->bqkgd', p.astype(v.dtype), v)

    out = lax.map(attend, qb)
    return out.transpose(1, 0, 2, 3, 4, 5).reshape(B, S, Hq * Dh)


def trunk(x, p, ln_mix_pre, w_in, q_norm, k_norm, na_rpb, w_na_branch, w_gqa_branch,
          w_mix_out, ln_mix_post, ln_mlp_pre, w_ff1, w_ff2, ln_mlp_post,
          ln_ple_pre, w_ple_gate, w_ple_proj, ln_ple_post):
    B, S, _ = x.shape
    splits = np.cumsum([NA_WIDTH, NA_WIDTH, NA_WIDTH, GQA_WIDTH, KV_WIDTH, KV_WIDTH, D_MODEL])
    for i in range(DEPTH):
        h = rms_norm(x, ln_mix_pre[i])
        proj = h @ w_in[i]
        na_q, na_k, na_v, g_q, g_k, g_v, gate_a, gate_b = jnp.split(proj, splits, axis=-1)
        a = neighbourhood_attention(na_q.reshape(B, S, NA_HEADS, HEAD_DIM),
                                    na_k.reshape(B, S, NA_HEADS, HEAD_DIM),
                                    na_v.reshape(B, S, NA_HEADS, HEAD_DIM), na_rpb[i])
        gq = axial_rope(rms_norm(g_q.reshape(B, S, GQA_HEADS, HEAD_DIM), q_norm[i]))
        gk = axial_rope(rms_norm(g_k.reshape(B, S, GQA_KV_HEADS, HEAD_DIM), k_norm[i]))
        b = gqa_attention(gq, gk, g_v.reshape(B, S, GQA_KV_HEADS, HEAD_DIM))
        merged = (jax.nn.sigmoid(gate_a) * (a @ w_na_branch[i])
                  + jax.nn.sigmoid(gate_b) * (b @ w_gqa_branch[i]))
        x = x + rms_norm(merged @ w_mix_out[i], ln_mix_post[i])
        h = rms_norm(x, ln_mlp_pre[i])
        f = jnp.square(jax.nn.relu(h @ w_ff1[i])) @ w_ff2[i]
        x = x + rms_norm(f, ln_mlp_post[i])
        gate = jax.nn.sigmoid(rms_norm(x, ln_ple_pre[i]) @ w_ple_gate[i])
        e = (p[i] @ w_ple_proj[i]) * gate
        x = x + rms_norm(e, ln_ple_post[i])
    return x


def setup_inputs(seed: int = 0) -> dict:
    key = jax.random.key(seed)
    ks = jax.random.split(key, 24)
    f32 = jnp.float32

    def dense(k, fan_in, fan_out):
        return jax.random.normal(k, (DEPTH, fan_in, fan_out), f32) * fan_in ** -0.5

    def gain(k, n):
        return 1.0 + 0.05 * jax.random.normal(k, (DEPTH, n), f32)

    return {
        "x_prompt": jax.random.normal(ks[0], (BATCH, SEQ, D_MODEL), f32),
        "x_sample": jax.random.normal(ks[1], (DEC_BATCH, DEC_SEQ, D_MODEL), f32),
        "p_prompt": jax.random.normal(ks[2], (DEPTH, BATCH, SEQ, PLE_DIM), f32),
        "p_sample": jax.random.normal(ks[3], (DEPTH, DEC_BATCH, DEC_SEQ, PLE_DIM), f32),
        "ln_mix_pre": gain(ks[4], D_MODEL),
        "w_in": dense(ks[5], D_MODEL, IN_COLS),
        "q_norm": gain(ks[6], HEAD_DIM),
        "k_norm": gain(ks[7], HEAD_DIM),
        "na_rpb": 0.1 * jax.random.normal(ks[8], (DEPTH, NA_HEADS, 2 * NA_ROWS_MAX - 1, 2 * NA_COLS - 1), f32),
        "w_na_branch": dense(ks[9], NA_WIDTH, D_MODEL),
        "w_gqa_branch": dense(ks[10], GQA_WIDTH, D_MODEL),
        "w_mix_out": dense(ks[11], D_MODEL, D_MODEL),
        "ln_mix_post": gain(ks[12], D_MODEL),
        "ln_mlp_pre": gain(ks[13], D_MODEL),
        "w_ff1": dense(ks[14], D_MODEL, D_FF),
        "w_ff2": dense(ks[15], D_FF, D_MODEL),
        "ln_mlp_post": gain(ks[16], D_MODEL),
        "ln_ple_pre": gain(ks[17], D_MODEL),
        "w_ple_gate": dense(ks[18], D_MODEL, D_MODEL),
        "w_ple_proj": dense(ks[19], PLE_DIM, D_MODEL),
        "ln_ple_post": gain(ks[20], D_MODEL),
    }


def reference(x_prompt, x_sample, p_prompt, p_sample, ln_mix_pre, w_in, q_norm, k_norm, na_rpb,
              w_na_branch, w_gqa_branch, w_mix_out, ln_mix_post, ln_mlp_pre, w_ff1, w_ff2,
              ln_mlp_post, ln_ple_pre, w_ple_gate, w_ple_proj, ln_ple_post):
    y_prompt = trunk(x_prompt, p_prompt, ln_mix_pre, w_in, q_norm, k_norm, na_rpb, w_na_branch,
                     w_gqa_branch, w_mix_out, ln_mix_post, ln_mlp_pre, w_ff1, w_ff2, ln_mlp_post,
                     ln_ple_pre, w_ple_gate, w_ple_proj, ln_ple_post)
    y_sample = trunk(x_sample, p_sample, ln_mix_pre, w_in, q_norm, k_norm, na_rpb, w_na_branch,
                     w_gqa_branch, w_mix_out, ln_mix_post, ln_mlp_pre, w_ff1, w_ff2, ln_mlp_post,
                     ln_ple_pre, w_ple_gate, w_ple_proj, ln_ple_post)
    return (y_prompt, y_sample)
```

```python
import functools

import jax
import jax.numpy as jnp
import numpy as np
from jax import lax
from jax.experimental import pallas as pl
from jax.experimental.pallas import tpu as pltpu

D_MODEL = 2048
GRID_W = 64
HEAD_DIM = 128
NA_HEADS = 8
GQA_HEADS = 8
GQA_KV_HEADS = 2
GQA_GROUP = GQA_HEADS // GQA_KV_HEADS
NA_WIDTH = NA_HEADS * HEAD_DIM
GQA_WIDTH = GQA_HEADS * HEAD_DIM
KV_WIDTH = GQA_KV_HEADS * HEAD_DIM
NA_ROWS_MAX = 8
NA_COLS = 16
D_FF = 4 * D_MODEL
PLE_DIM = 256
ROPE_THETA = 10000.0
EPS = 1e-6
NEG_INF = -1e30
ATTN_SCALE = HEAD_DIM ** -0.5

RPB_ROWS = 2 * NA_ROWS_MAX - 1
RPB_COLS = 2 * NA_COLS - 1

VMEM_LIMIT_BYTES = 56 * 1024 * 1024

BF16 = jnp.bfloat16
F32 = jnp.float32


def _params(*semantics):
    return pltpu.CompilerParams(dimension_semantics=semantics, vmem_limit_bytes=VMEM_LIMIT_BYTES)


def _resident(shape):
    zeros = (0,) * len(shape)
    return pl.BlockSpec(shape, lambda *_: zeros, pipeline_mode=pl.Buffered(1))


def _rms(x, gain):
    ms = jnp.mean(x * x, axis=-1, keepdims=True)
    return x * lax.rsqrt(ms + EPS) * gain


def _na_bias_kernel(rpb_ref, o_ref):
    h = pl.program_id(0)
    qc = lax.broadcasted_iota(jnp.int32, (GRID_W, 2 * GRID_W), 0)
    lane = lax.broadcasted_iota(jnp.int32, (GRID_W, 2 * GRID_W), 1)
    upper = lane >= GRID_W
    kc = jnp.where(upper, lane - GRID_W, lane)
    dc = jnp.clip(kc - qc, -(NA_COLS - 1), NA_COLS - 1) + (NA_COLS - 1)
    col_start = jnp.clip(qc - NA_COLS // 2, 0, GRID_W - NA_COLS)
    col_ok = (kc >= col_start) & (kc < col_start + NA_COLS)
    base = h * (RPB_ROWS * RPB_COLS)
    pairs = []
    for t in range(RPB_ROWS - 1):
        acc = jnp.zeros((GRID_W, 2 * GRID_W), F32)
        for d in range(RPB_COLS):
            lo = rpb_ref[base + t * RPB_COLS + d]
            hi = rpb_ref[base + (t + 1) * RPB_COLS + d]
            acc = jnp.where(dc == d, jnp.where(upper, hi, lo), acc)
        pairs.append(jnp.where(col_ok, acc, NEG_INF))
    for i in range(NA_ROWS_MAX):
        for jp in range(NA_ROWS_MAX // 2):
            t = (2 * jp - i) + (NA_ROWS_MAX - 1)
            o_ref[0, i, :, jp * 2 * GRID_W:(jp + 1) * 2 * GRID_W] = pairs[t]


def _na_bias_table(rpb):
    return pl.pallas_call(
        _na_bias_kernel,
        grid=(NA_HEADS,),
        in_specs=[pl.BlockSpec(memory_space=pltpu.SMEM)],
        out_specs=pl.BlockSpec((1, NA_ROWS_MAX, GRID_W, NA_ROWS_MAX * GRID_W), lambda h: (h, 0, 0, 0)),
        out_shape=jax.ShapeDtypeStruct((NA_HEADS, NA_ROWS_MAX, GRID_W, NA_ROWS_MAX * GRID_W), F32),
        compiler_params=_params("arbitrary"),
        name="na_bias_table",
    )(rpb.reshape(-1))


INPROJ_TM = 512
INPROJ_TN = 512
_NA_TILES = 3 * NA_WIDTH // INPROJ_TN
_NAQ_TILES = NA_WIDTH // INPROJ_TN
_GQ_TILES = GQA_WIDTH // INPROJ_TN
_GATE_TILES = 2 * D_MODEL // INPROJ_TN
_KV_TILE = _NA_TILES + _GQ_TILES
_GATE_TILE0 = _KV_TILE + 1
assert 2 * KV_WIDTH == INPROJ_TN
_HEADS_PER_TILE = INPROJ_TN // HEAD_DIM


def _rope(y, cos, sin_lo, sin_hi):
    return y * cos + pltpu.roll(y, 32, 1) * sin_hi + pltpu.roll(y, HEAD_DIM - 32, 1) * sin_lo


def _inproj_kernel(x_ref, g_ref, w_ref, qn_ref, kn_ref, cos_ref, slo_ref, shi_ref,
                   na_ref, gq_ref, kv_ref, gt_ref, h_ref):
    j = pl.program_id(1)

    @pl.when(j == 0)
    def _():
        h_ref[...] = _rms(x_ref[...], g_ref[...]).astype(BF16)

    acc = jnp.dot(h_ref[...], w_ref[...], preferred_element_type=F32)

    @pl.when(j < _NAQ_TILES)
    def _():
        na_ref[...] = (acc * ATTN_SCALE).astype(BF16)

    @pl.when((j >= _NAQ_TILES) & (j < _NA_TILES))
    def _():
        na_ref[...] = acc.astype(BF16)

    @pl.when((j >= _NA_TILES) & (j < _KV_TILE))
    def _():
        for hh in range(_HEADS_PER_TILE):
            sl = slice(hh * HEAD_DIM, (hh + 1) * HEAD_DIM)
            y = _rope(_rms(acc[:, sl], qn_ref[...]), cos_ref[...], slo_ref[...], shi_ref[...])
            gq_ref[:, sl] = (y * ATTN_SCALE).astype(BF16)

    @pl.when(j == _KV_TILE)
    def _():
        for hh in range(GQA_KV_HEADS):
            sl = slice(hh * HEAD_DIM, (hh + 1) * HEAD_DIM)
            y = _rope(_rms(acc[:, sl], kn_ref[...]), cos_ref[...], slo_ref[...], shi_ref[...])
            kv_ref[:, sl] = y.astype(BF16)
        kv_ref[:, KV_WIDTH:] = acc[:, KV_WIDTH:].astype(BF16)

    @pl.when(j >= _GATE_TILE0)
    def _():
        gt_ref[...] = jax.nn.sigmoid(acc).astype(BF16)


def _in_projection(x2d, seq_len, ln_pre, w_in, q_norm, k_norm, cos, sin_lo, sin_hi):
    tokens = x2d.shape[0]
    tm, tn = INPROJ_TM, INPROJ_TN
    n_col = w_in.shape[1] // tn
    seq_tiles = seq_len // tm
    pos_spec = pl.BlockSpec((tm, HEAD_DIM), lambda i, j: (i % seq_tiles, 0))
    vec = lambda n: pl.BlockSpec((1, n), lambda i, j: (0, 0))
    return pl.pallas_call(
        _inproj_kernel,
        grid=(tokens // tm, n_col),
        in_specs=[
            pl.BlockSpec((tm, D_MODEL), lambda i, j: (i, 0)),
            vec(D_MODEL),
            pl.BlockSpec((D_MODEL, tn), lambda i, j: (0, j)),
            vec(HEAD_DIM), vec(HEAD_DIM),
            pos_spec, pos_spec, pos_spec,
        ],
        out_specs=[
            pl.BlockSpec((tm, tn), lambda i, j: (i, jnp.minimum(j, _NA_TILES - 1))),
            pl.BlockSpec((tm, tn), lambda i, j: (i, jnp.clip(j - _NA_TILES, 0, _GQ_TILES - 1))),
            pl.BlockSpec((tm, tn), lambda i, j: (i, 0)),
            pl.BlockSpec((tm, tn), lambda i, j: (i, jnp.clip(j - _GATE_TILE0, 0, _GATE_TILES - 1))),
        ],
        out_shape=[
            jax.ShapeDtypeStruct((tokens, 3 * NA_WIDTH), BF16),
            jax.ShapeDtypeStruct((tokens, GQA_WIDTH), BF16),
            jax.ShapeDtypeStruct((tokens, 2 * KV_WIDTH), BF16),
            jax.ShapeDtypeStruct((tokens, 2 * D_MODEL), BF16),
        ],
        scratch_shapes=[pltpu.VMEM((tm, D_MODEL), BF16)],
        compiler_params=_params("parallel", "arbitrary"),
        name="in_projection",
    )(x2d, ln_pre, w_in, q_norm, k_norm, cos, sin_lo, sin_hi)


def _na_kernel(q_ref, k_ref, v_ref, bias_ref, o_ref, *, rows):
    kr = min(NA_ROWS_MAX, rows)

    def body(r, carry):
        row_start = jnp.clip(r - kr // 2, 0, rows - kr)
        q = q_ref[0, pl.ds(pl.multiple_of(r * GRID_W, GRID_W), GRID_W), :]
        koff = pl.multiple_of(row_start * GRID_W, GRID_W)
        k = k_ref[0, pl.ds(koff, kr * GRID_W), :]
        v = v_ref[0, pl.ds(koff, kr * GRID_W), :]
        s = lax.dot_general(q, k, (((1,), (1,)), ((), ())), preferred_element_type=F32)
        s = s + bias_ref[0, r - row_start]
        m = jnp.max(s, axis=-1, keepdims=True)
        p = jnp.exp(s - m)
        l = jnp.sum(p, axis=-1, keepdims=True)
        o = jnp.dot(p.astype(BF16), v, preferred_element_type=F32)
        o_ref[0, pl.ds(pl.multiple_of(r * GRID_W, GRID_W), GRID_W), :] = (o / l).astype(BF16)
        return carry

    lax.fori_loop(0, rows, body, 0)


def _neighbourhood_attention(na3d, bias):
    batch, seq_len, _ = na3d.shape
    rows = seq_len // GRID_W
    assert rows >= NA_ROWS_MAX
    head_spec = lambda off: pl.BlockSpec((1, seq_len, HEAD_DIM), lambda h, b: (b, 0, h + off))
    return pl.pallas_call(
        functools.partial(_na_kernel, rows=rows),
        grid=(NA_HEADS, batch),
        in_specs=[
            head_spec(0), head_spec(NA_HEADS), head_spec(2 * NA_HEADS),
            pl.BlockSpec((1, NA_ROWS_MAX, GRID_W, NA_ROWS_MAX * GRID_W), lambda h, b: (h, 0, 0, 0)),
        ],
        out_specs=pl.BlockSpec((1, seq_len, HEAD_DIM), lambda h, b: (b, 0, h)),
        out_shape=jax.ShapeDtypeStruct((batch, seq_len, NA_WIDTH), BF16),
        compiler_params=_params("parallel", "parallel"),
        name="neighbourhood_attention",
    )(na3d, na3d, na3d, bias)


GQA_TQ = 256


def _gqa_kernel(q_ref, k_ref, v_ref, o_ref):
    k = k_ref[0]
    v = v_ref[0]
    for g in range(GQA_GROUP):
        sl = slice(g * HEAD_DIM, (g + 1) * HEAD_DIM)
        q = q_ref[0, :, sl]
        s = lax.dot_general(q, k, (((1,), (1,)), ((), ())), preferred_element_type=F32)
        m = jnp.max(s, axis=-1, keepdims=True)
        p = jnp.exp(s - m)
        l = jnp.sum(p, axis=-1, keepdims=True)
        o = jnp.dot(p.astype(BF16), v, preferred_element_type=F32)
        o_ref[0, :, sl] = (o / l).astype(BF16)


def _gqa_attention(q3d, kv3d):
    batch, seq_len, _ = q3d.shape
    tq = GQA_TQ
    group_w = GQA_GROUP * HEAD_DIM
    return pl.pallas_call(
        _gqa_kernel,
        grid=(batch, GQA_KV_HEADS, seq_len // tq),
        in_specs=[
            pl.BlockSpec((1, tq, group_w), lambda b, kv, qi: (b, qi, kv)),
            pl.BlockSpec((1, seq_len, HEAD_DIM), lambda b, kv, qi: (b, 0, kv)),
            pl.BlockSpec((1, seq_len, HEAD_DIM), lambda b, kv, qi: (b, 0, GQA_KV_HEADS + kv)),
        ],
        out_specs=pl.BlockSpec((1, tq, group_w), lambda b, kv, qi: (b, qi, kv)),
        out_shape=jax.ShapeDtypeStruct((batch, seq_len, GQA_WIDTH), BF16),
        compiler_params=_params("parallel", "parallel", "arbitrary"),
        name="gqa_attention",
    )(q3d, kv3d, kv3d)


MERGE_TM = 256


def _merge_kernel(a_ref, b_ref, ga_ref, gb_ref, x_ref, wna_ref, wgq_ref, wmix_ref,
                  lnpost_ref, lnmlp_ref, x1_ref, h2_ref):
    pa = jnp.dot(a_ref[...], wna_ref[...], preferred_element_type=F32)
    pb = jnp.dot(b_ref[...], wgq_ref[...], preferred_element_type=F32)
    merged = ga_ref[...].astype(F32) * pa + gb_ref[...].astype(F32) * pb
    y = jnp.dot(merged.astype(BF16), wmix_ref[...], preferred_element_type=F32)
    x1 = x_ref[...] + _rms(y, lnpost_ref[...])
    x1_ref[...] = x1
    h2_ref[...] = _rms(x1, lnmlp_ref[...]).astype(BF16)


def _merge(a, b, gates, x2d, w_na, w_gq, w_mix, ln_post, ln_mlp):
    tokens = x2d.shape[0]
    tm = MERGE_TM
    row = lambda n, c=0: pl.BlockSpec((tm, n), lambda i: (i, c))
    return pl.pallas_call(
        _merge_kernel,
        grid=(tokens // tm,),
        in_specs=[
            row(NA_WIDTH), row(GQA_WIDTH), row(D_MODEL, 0), row(D_MODEL, 1), row(D_MODEL),
            _resident((NA_WIDTH, D_MODEL)), _resident((GQA_WIDTH, D_MODEL)),
            _resident((D_MODEL, D_MODEL)), _resident((1, D_MODEL)), _resident((1, D_MODEL)),
        ],
        out_specs=[row(D_MODEL), row(D_MODEL)],
        out_shape=[jax.ShapeDtypeStruct((tokens, D_MODEL), F32),
                   jax.ShapeDtypeStruct((tokens, D_MODEL), BF16)],
        compiler_params=_params("parallel"),
        name="branch_merge",
    )(a, b, gates, gates, x2d, w_na, w_gq, w_mix, ln_post, ln_mlp)


MLP_TM = 1024
MLP_TK = 512


def _mlp_kernel(h_ref, w1_ref, w2_ref, f_ref):
    k = pl.program_id(1)
    u = jnp.maximum(jnp.dot(h_ref[...], w1_ref[...], preferred_element_type=F32), 0.0)
    contrib = jnp.dot((u * u).astype(BF16), w2_ref[...], preferred_element_type=F32)

    @pl.when(k == 0)
    def _():
        f_ref[...] = contrib

    @pl.when(k > 0)
    def _():
        f_ref[...] += contrib


def _mlp(h2, w1, w2):
    tokens = h2.shape[0]
    tm, tk = MLP_TM, MLP_TK
    return pl.pallas_call(
        _mlp_kernel,
        grid=(tokens // tm, D_FF // tk),
        in_specs=[
            pl.BlockSpec((tm, D_MODEL), lambda i, k: (i, 0)),
            pl.BlockSpec((D_MODEL, tk), lambda i, k: (0, k)),
            pl.BlockSpec((tk, D_MODEL), lambda i, k: (k, 0)),
        ],
        out_specs=pl.BlockSpec((tm, D_MODEL), lambda i, k: (i, 0)),
        out_shape=jax.ShapeDtypeStruct((tokens, D_MODEL), F32),
        compiler_params=_params("parallel", "arbitrary"),
        name="mlp",
    )(h2, w1, w2)


PLE_TM = 256


def _ple_kernel(x1_ref, f_ref, p_ref, wg_ref, wp_ref, lnpost_ref, lnpre_ref, lnple_ref, o_ref):
    x2 = x1_ref[...] + _rms(f_ref[...], lnpost_ref[...])
    hg = _rms(x2, lnpre_ref[...]).astype(BF16)
    gate = jax.nn.sigmoid(jnp.dot(hg, wg_ref[...], preferred_element_type=F32))
    e = jnp.dot(p_ref[...].astype(BF16), wp_ref[...], preferred_element_type=F32) * gate
    o_ref[...] = x2 + _rms(e, lnple_ref[...])


def _ple(x1, f, p2d, w_gate, w_proj, ln_mlp_post, ln_ple_pre, ln_ple_post):
    tokens = x1.shape[0]
    tm = PLE_TM
    row = lambda n: pl.BlockSpec((tm, n), lambda i: (i, 0))
    return pl.pallas_call(
        _ple_kernel,
        grid=(tokens // tm,),
        in_specs=[
            row(D_MODEL), row(D_MODEL), row(PLE_DIM),
            _resident((D_MODEL, D_MODEL)), _resident((PLE_DIM, D_MODEL)),
            _resident((1, D_MODEL)), _resident((1, D_MODEL)), _resident((1, D_MODEL)),
        ],
        out_specs=row(D_MODEL),
        out_shape=jax.ShapeDtypeStruct((tokens, D_MODEL), F32),
        compiler_params=_params("parallel"),
        name="ple",
    )(x1, f, p2d, w_gate, w_proj, ln_mlp_post, ln_ple_pre, ln_ple_post)


def _rope_tables(seq_len):
    half = HEAD_DIM // 2
    quarter = half // 2
    t = jnp.arange(seq_len)
    freqs = ROPE_THETA ** (-jnp.arange(0, half, 2, dtype=F32) / half)

    def axis_tables(pos):
        ang = pos.astype(F32)[:, None] * freqs[None, :]
        cos, sin = jnp.cos(ang), jnp.sin(ang)
        zero = jnp.zeros_like(sin)
        return (jnp.concatenate([cos, cos], -1), jnp.concatenate([-sin, zero], -1),
                jnp.concatenate([zero, sin], -1))

    row_t = axis_tables(t // GRID_W)
    col_t = axis_tables(t % GRID_W)
    assert quarter * 4 == HEAD_DIM
    return tuple(jnp.concatenate([a, b], -1) for a, b in zip(row_t, col_t))


def _trunk(x, p, weights, bias_tables, rope):
    batch, seq_len, _ = x.shape
    tokens = batch * seq_len
    x2d = x.reshape(tokens, D_MODEL)
    for i, w in enumerate(weights):
        na, gq, kv, gates = _in_projection(x2d, seq_len, w["ln_mix_pre"], w["w_in"], w["q_norm"],
                                           w["k_norm"], *rope)
        a = _neighbourhood_attention(na.reshape(batch, seq_len, 3 * NA_WIDTH), bias_tables[i])
        b = _gqa_attention(gq.reshape(batch, seq_len, GQA_WIDTH),
                           kv.reshape(batch, seq_len, 2 * KV_WIDTH))
        x1, h2 = _merge(a.reshape(tokens, NA_WIDTH), b.reshape(tokens, GQA_WIDTH), gates, x2d,
                        w["w_na_branch"], w["w_gqa_branch"], w["w_mix_out"], w["ln_mix_post"],
                        w["ln_mlp_pre"])
        f = _mlp(h2, w["w_ff1"], w["w_ff2"])
        x2d = _ple(x1, f, p[i].reshape(tokens, PLE_DIM), w["w_ple_gate"], w["w_ple_proj"],
                   w["ln_mlp_post"], w["ln_ple_pre"], w["ln_ple_post"])
    return x2d.reshape(batch, seq_len, D_MODEL)


_MATRICES = ("w_in", "w_na_branch", "w_gqa_branch", "w_mix_out", "w_ff1", "w_ff2", "w_ple_gate",
             "w_ple_proj")
_VECTORS = ("ln_mix_pre", "q_norm", "k_norm", "ln_mix_post", "ln_mlp_pre", "ln_mlp_post",
            "ln_ple_pre", "ln_ple_post")


def kernel(x_prompt, x_sample, p_prompt, p_sample, ln_mix_pre, w_in, q_norm, k_norm, na_rpb, w_na_branch, w_gqa_branch, w_mix_out, ln_mix_post, ln_mlp_pre, w_ff1, w_ff2, ln_mlp_post, ln_ple_pre, w_ple_gate, w_ple_proj, ln_ple_post):
    named = dict(ln_mix_pre=ln_mix_pre, w_in=w_in, q_norm=q_norm, k_norm=k_norm,
                 w_na_branch=w_na_branch, w_gqa_branch=w_gqa_branch, w_mix_out=w_mix_out,
                 ln_mix_post=ln_mix_post, ln_mlp_pre=ln_mlp_pre, w_ff1=w_ff1, w_ff2=w_ff2,
                 ln_mlp_post=ln_mlp_post, ln_ple_pre=ln_ple_pre, w_ple_gate=w_ple_gate,
                 w_ple_proj=w_ple_proj, ln_ple_post=ln_ple_post)
    depth = w_in.shape[0]
    weights = []
    for i in range(depth):
        layer = {n: named[n][i].astype(BF16) for n in _MATRICES}
        layer.update({n: named[n][i].reshape(1, -1) for n in _VECTORS})
        weights.append(layer)
    bias_tables = [_na_bias_table(na_rpb[i]) for i in range(depth)]
    outs = []
    for x, p in ((x_prompt, p_prompt), (x_sample, p_sample)):
        rope = _rope_tables(x.shape[1])
        outs.append(_trunk(x, p, weights, bias_tables, rope))
    return tuple(outs)
```

```python
import functools

import jax
import jax.numpy as jnp
import numpy as np
from jax import lax
from jax.experimental import pallas as pl
from jax.experimental.pallas import tpu as pltpu

D_MODEL = 2048
GRID_W = 64
HEAD_DIM = 128
NA_HEADS = 8
GQA_HEADS = 8
GQA_KV_HEADS = 2
GQA_GROUP = GQA_HEADS // GQA_KV_HEADS
NA_WIDTH = NA_HEADS * HEAD_DIM
GQA_WIDTH = GQA_HEADS * HEAD_DIM
KV_WIDTH = GQA_KV_HEADS * HEAD_DIM
NA_ROWS_MAX = 8
NA_COLS = 16
D_FF = 4 * D_MODEL
PLE_DIM = 256
ROPE_THETA = 10000.0
EPS = 1e-6
NEG_INF = -1e30
ATTN_SCALE = HEAD_DIM ** -0.5

RPB_ROWS = 2 * NA_ROWS_MAX - 1
RPB_COLS = 2 * NA_COLS - 1

VMEM_LIMIT_BYTES = 56 * 1024 * 1024

BF16 = jnp.bfloat16
F32 = jnp.float32


def _params(*semantics):
    return pltpu.CompilerParams(dimension_semantics=semantics, vmem_limit_bytes=VMEM_LIMIT_BYTES)


def _resident(shape):
    zeros = (0,) * len(shape)
    return pl.BlockSpec(shape, lambda *_: zeros, pipeline_mode=pl.Buffered(1))


def _rms(x, gain):
    ms = jnp.mean(x * x, axis=-1, keepdims=True)
    return x * lax.rsqrt(ms + EPS) * gain


NA_QROWS = 4
NA_KROWS = NA_QROWS + NA_ROWS_MAX
NA_BLOCK_TYPES = 3
NA_BQ = NA_QROWS * GRID_W
NA_BK = NA_KROWS * GRID_W


def _na_block_rows(block_type, rr):
    if block_type == 0:
        return 0, -rr
    if block_type == 1:
        return rr, -(NA_ROWS_MAX // 2) - rr
    return NA_QROWS, -NA_ROWS_MAX - rr


def _na_bias_kernel(rpb_ref, o_ref):
    h = pl.program_id(0)
    qc = lax.broadcasted_iota(jnp.int32, (GRID_W, 2 * GRID_W), 0)
    lane = lax.broadcasted_iota(jnp.int32, (GRID_W, 2 * GRID_W), 1)
    upper = lane >= GRID_W
    kc = jnp.where(upper, lane - GRID_W, lane)
    dc = jnp.clip(kc - qc, -(NA_COLS - 1), NA_COLS - 1) + (NA_COLS - 1)
    col_start = jnp.clip(qc - NA_COLS // 2, 0, GRID_W - NA_COLS)
    col_ok = (kc >= col_start) & (kc < col_start + NA_COLS)
    base = h * (RPB_ROWS * RPB_COLS)
    neg = jnp.full((GRID_W, 2 * GRID_W), NEG_INF, F32)
    pairs = []
    for t in range(RPB_ROWS - 1):
        acc = jnp.zeros((GRID_W, 2 * GRID_W), F32)
        for d in range(RPB_COLS):
            lo = rpb_ref[base + t * RPB_COLS + d]
            hi = rpb_ref[base + (t + 1) * RPB_COLS + d]
            acc = jnp.where(dc == d, jnp.where(upper, hi, lo), acc)
        pairs.append(jnp.where(col_ok, acc, NEG_INF))
    for block_type in range(NA_BLOCK_TYPES):
        for rr in range(NA_QROWS):
            j0, dr0 = _na_block_rows(block_type, rr)
            for jp in range(NA_KROWS // 2):
                lo_ok = j0 <= 2 * jp < j0 + NA_ROWS_MAX
                hi_ok = j0 <= 2 * jp + 1 < j0 + NA_ROWS_MAX
                if not (lo_ok or hi_ok):
                    tile = neg
                else:
                    tile = pairs[dr0 + 2 * jp + (NA_ROWS_MAX - 1)]
                    if not lo_ok:
                        tile = jnp.where(upper, tile, NEG_INF)
                    if not hi_ok:
                        tile = jnp.where(upper, NEG_INF, tile)
                o_ref[0, block_type, rr * GRID_W:(rr + 1) * GRID_W,
                      jp * 2 * GRID_W:(jp + 1) * 2 * GRID_W] = tile


def _na_bias_table(rpb):
    shape = (NA_HEADS, NA_BLOCK_TYPES, NA_BQ, NA_BK)
    return pl.pallas_call(
        _na_bias_kernel,
        grid=(NA_HEADS,),
        in_specs=[pl.BlockSpec(memory_space=pltpu.SMEM)],
        out_specs=pl.BlockSpec((1,) + shape[1:], lambda h: (h, 0, 0, 0)),
        out_shape=jax.ShapeDtypeStruct(shape, F32),
        compiler_params=_params("arbitrary"),
        name="na_bias_table",
    )(rpb.reshape(-1))


INPROJ_TM = 1024
INPROJ_TN = 512
INPROJ_CHUNK = 256


def _row_chunks():
    return [slice(r, r + INPROJ_CHUNK) for r in range(0, INPROJ_TM, INPROJ_CHUNK)]
_NA_TILES = 3 * NA_WIDTH // INPROJ_TN
_NAQ_TILES = NA_WIDTH // INPROJ_TN
_GQ_TILES = GQA_WIDTH // INPROJ_TN
_GATE_TILES = 2 * D_MODEL // INPROJ_TN
_KV_TILE = _NA_TILES + _GQ_TILES
_GATE_TILE0 = _KV_TILE + 1
assert 2 * KV_WIDTH == INPROJ_TN
_HEADS_PER_TILE = INPROJ_TN // HEAD_DIM


def _rope(y, cos, sin_lo, sin_hi):
    return y * cos + pltpu.roll(y, 32, 1) * sin_hi + pltpu.roll(y, HEAD_DIM - 32, 1) * sin_lo


def _inproj_kernel(x_ref, g_ref, w_ref, qn_ref, kn_ref, cos_ref, slo_ref, shi_ref,
                   na_ref, gq_ref, kv_ref, gt_ref, h_ref):
    j = pl.program_id(1)

    @pl.when(j == 0)
    def _():
        for rows in _row_chunks():
            h_ref[rows, :] = _rms(x_ref[rows, :], g_ref[...]).astype(BF16)

    def project(epilogue):
        for rows in _row_chunks():
            epilogue(rows, jnp.dot(h_ref[rows, :], w_ref[...], preferred_element_type=F32))

    def norm_rope(rows, acc, hh, gain_ref):
        sl = slice(hh * HEAD_DIM, (hh + 1) * HEAD_DIM)
        return _rope(_rms(acc[:, sl], gain_ref[...]), cos_ref[rows, :], slo_ref[rows, :],
                     shi_ref[rows, :])

    @pl.when(j < _NAQ_TILES)
    def _():
        def epilogue(rows, acc):
            na_ref[rows, :] = (acc * ATTN_SCALE).astype(BF16)
        project(epilogue)

    @pl.when((j >= _NAQ_TILES) & (j < _NA_TILES))
    def _():
        def epilogue(rows, acc):
            na_ref[rows, :] = acc.astype(BF16)
        project(epilogue)

    @pl.when((j >= _NA_TILES) & (j < _KV_TILE))
    def _():
        def epilogue(rows, acc):
            for hh in range(_HEADS_PER_TILE):
                y = norm_rope(rows, acc, hh, qn_ref)
                gq_ref[rows, hh * HEAD_DIM:(hh + 1) * HEAD_DIM] = (y * ATTN_SCALE).astype(BF16)
        project(epilogue)

    @pl.when(j == _KV_TILE)
    def _():
        def epilogue(rows, acc):
            for hh in range(GQA_KV_HEADS):
                y = norm_rope(rows, acc, hh, kn_ref)
                kv_ref[rows, hh * HEAD_DIM:(hh + 1) * HEAD_DIM] = y.astype(BF16)
            kv_ref[rows, KV_WIDTH:] = acc[:, KV_WIDTH:].astype(BF16)
        project(epilogue)

    @pl.when(j >= _GATE_TILE0)
    def _():
        def epilogue(rows, acc):
            gt_ref[rows, :] = jax.nn.sigmoid(acc).astype(BF16)
        project(epilogue)


def _in_projection(x2d, seq_len, ln_pre, w_in, q_norm, k_norm, cos, sin_lo, sin_hi):
    tokens = x2d.shape[0]
    tm, tn = INPROJ_TM, INPROJ_TN
    n_col = w_in.shape[1] // tn
    seq_tiles = seq_len // tm
    pos_spec = pl.BlockSpec((tm, HEAD_DIM), lambda i, j: (i % seq_tiles, 0))
    vec = lambda n: pl.BlockSpec((1, n), lambda i, j: (0, 0))
    return pl.pallas_call(
        _inproj_kernel,
        grid=(tokens // tm, n_col),
        in_specs=[
            pl.BlockSpec((tm, D_MODEL), lambda i, j: (i, 0)),
            vec(D_MODEL),
            pl.BlockSpec((D_MODEL, tn), lambda i, j: (0, j)),
            vec(HEAD_DIM), vec(HEAD_DIM),
            pos_spec, pos_spec, pos_spec,
        ],
        out_specs=[
            pl.BlockSpec((tm, tn), lambda i, j: (i, jnp.minimum(j, _NA_TILES - 1))),
            pl.BlockSpec((tm, tn), lambda i, j: (i, jnp.clip(j - _NA_TILES, 0, _GQ_TILES - 1))),
            pl.BlockSpec((tm, tn), lambda i, j: (i, 0)),
            pl.BlockSpec((tm, tn), lambda i, j: (i, jnp.clip(j - _GATE_TILE0, 0, _GATE_TILES - 1))),
        ],
        out_shape=[
            jax.ShapeDtypeStruct((tokens, 3 * NA_WIDTH), BF16),
            jax.ShapeDtypeStruct((tokens, GQA_WIDTH), BF16),
            jax.ShapeDtypeStruct((tokens, 2 * KV_WIDTH), BF16),
            jax.ShapeDtypeStruct((tokens, 2 * D_MODEL), BF16),
        ],
        scratch_shapes=[pltpu.VMEM((tm, D_MODEL), BF16)],
        compiler_params=_params("parallel", "arbitrary"),
        name="in_projection",
    )(x2d, ln_pre, w_in, q_norm, k_norm, cos, sin_lo, sin_hi)


def _na_kernel(q_ref, k_ref, v_ref, bias_ref, o_ref, *, rows):
    n_blocks = rows // NA_QROWS

    def body(blk, carry):
        key_row0 = jnp.clip(NA_QROWS * blk - NA_ROWS_MAX // 2, 0, rows - NA_KROWS)
        block_type = jnp.where(blk == 0, 0, jnp.where(blk == n_blocks - 1, 2, 1))
        qoff = pl.multiple_of(blk * NA_BQ, NA_BQ)
        koff = pl.multiple_of(key_row0 * GRID_W, GRID_W)
        q = q_ref[0, pl.ds(qoff, NA_BQ), :]
        k = k_ref[0, pl.ds(koff, NA_BK), :]
        v = v_ref[0, pl.ds(koff, NA_BK), :]
        s = lax.dot_general(q, k, (((1,), (1,)), ((), ())), preferred_element_type=F32)
        s = s + bias_ref[0, block_type]
        m = jnp.max(s, axis=-1, keepdims=True)
        p = jnp.exp(s - m)
        l = jnp.sum(p, axis=-1, keepdims=True)
        o = jnp.dot(p.astype(BF16), v, preferred_element_type=F32)
        o_ref[0, pl.ds(qoff, NA_BQ), :] = (o / l).astype(BF16)
        return carry

    lax.fori_loop(0, n_blocks, body, 0, unroll=2)


def _neighbourhood_attention(na3d, bias):
    batch, seq_len, _ = na3d.shape
    rows = seq_len // GRID_W
    assert rows % NA_QROWS == 0 and rows >= NA_KROWS + NA_QROWS
    head_spec = lambda off: pl.BlockSpec((1, seq_len, HEAD_DIM), lambda h, b: (b, 0, h + off))
    return pl.pallas_call(
        functools.partial(_na_kernel, rows=rows),
        grid=(NA_HEADS, batch),
        in_specs=[
            head_spec(0), head_spec(NA_HEADS), head_spec(2 * NA_HEADS),
            pl.BlockSpec((1, NA_BLOCK_TYPES, NA_BQ, NA_BK), lambda h, b: (h, 0, 0, 0)),
        ],
        out_specs=pl.BlockSpec((1, seq_len, HEAD_DIM), lambda h, b: (b, 0, h)),
        out_shape=jax.ShapeDtypeStruct((batch, seq_len, NA_WIDTH), BF16),
        compiler_params=_params("parallel", "parallel"),
        name="neighbourhood_attention",
    )(na3d, na3d, na3d, bias)


GQA_TQ = 256


def _gqa_kernel(q_ref, k_ref, v_ref, o_ref):
    k = k_ref[0]
    v = v_ref[0]
    for g in range(GQA_GROUP):
        sl = slice(g * HEAD_DIM, (g + 1) * HEAD_DIM)
        q = q_ref[0, :, sl]
        s = lax.dot_general(q, k, (((1,), (1,)), ((), ())), preferred_element_type=F32)
        m = jnp.max(s, axis=-1, keepdims=True)
        p = jnp.exp(s - m)
        l = jnp.sum(p, axis=-1, keepdims=True)
        o = jnp.dot(p.astype(BF16), v, preferred_element_type=F32)
        o_ref[0, :, sl] = (o / l).astype(BF16)


def _gqa_attention(q3d, kv3d):
    batch, seq_len, _ = q3d.shape
    tq = GQA_TQ
    group_w = GQA_GROUP * HEAD_DIM
    return pl.pallas_call(
        _gqa_kernel,
        grid=(batch, GQA_KV_HEADS, seq_len // tq),
        in_specs=[
            pl.BlockSpec((1, tq, group_w), lambda b, kv, qi: (b, qi, kv)),
            pl.BlockSpec((1, seq_len, HEAD_DIM), lambda b, kv, qi: (b, 0, kv)),
            pl.BlockSpec((1, seq_len, HEAD_DIM), lambda b, kv, qi: (b, 0, GQA_KV_HEADS + kv)),
        ],
        out_specs=pl.BlockSpec((1, tq, group_w), lambda b, kv, qi: (b, qi, kv)),
        out_shape=jax.ShapeDtypeStruct((batch, seq_len, GQA_WIDTH), BF16),
        compiler_params=_params("parallel", "parallel", "arbitrary"),
        name="gqa_attention",
    )(q3d, kv3d, kv3d)


MERGE_TM = 256


def _merge_kernel(a_ref, b_ref, ga_ref, gb_ref, x_ref, wna_ref, wgq_ref, wmix_ref,
                  lnpost_ref, lnmlp_ref, x1_ref, h2_ref):
    pa = jnp.dot(a_ref[...], wna_ref[...], preferred_element_type=F32)
    pb = jnp.dot(b_ref[...], wgq_ref[...], preferred_element_type=F32)
    merged = ga_ref[...].astype(F32) * pa + gb_ref[...].astype(F32) * pb
    y = jnp.dot(merged.astype(BF16), wmix_ref[...], preferred_element_type=F32)
    x1 = x_ref[...] + _rms(y, lnpost_ref[...])
    x1_ref[...] = x1
    h2_ref[...] = _rms(x1, lnmlp_ref[...]).astype(BF16)


def _merge(a, b, gates, x2d, w_na, w_gq, w_mix, ln_post, ln_mlp):
    tokens = x2d.shape[0]
    tm = MERGE_TM
    row = lambda n, c=0: pl.BlockSpec((tm, n), lambda i: (i, c))
    return pl.pallas_call(
        _merge_kernel,
        grid=(tokens // tm,),
        in_specs=[
            row(NA_WIDTH), row(GQA_WIDTH), row(D_MODEL, 0), row(D_MODEL, 1), row(D_MODEL),
            _resident((NA_WIDTH, D_MODEL)), _resident((GQA_WIDTH, D_MODEL)),
            _resident((D_MODEL, D_MODEL)), _resident((1, D_MODEL)), _resident((1, D_MODEL)),
        ],
        out_specs=[row(D_MODEL), row(D_MODEL)],
        out_shape=[jax.ShapeDtypeStruct((tokens, D_MODEL), F32),
                   jax.ShapeDtypeStruct((tokens, D_MODEL), BF16)],
        compiler_params=_params("parallel"),
        name="branch_merge",
    )(a, b, gates, gates, x2d, w_na, w_gq, w_mix, ln_post, ln_mlp)


MLP_TM = 1024
MLP_TK = 512


def _mlp_kernel(h_ref, w1_ref, w2_ref, f_ref):
    k = pl.program_id(1)
    u = jnp.maximum(jnp.dot(h_ref[...], w1_ref[...], preferred_element_type=F32), 0.0)
    contrib = jnp.dot((u * u).astype(BF16), w2_ref[...], preferred_element_type=F32)

    @pl.when(k == 0)
    def _():
        f_ref[...] = contrib

    @pl.when(k > 0)
    def _():
        f_ref[...] += contrib


def _mlp(h2, w1, w2):
    tokens = h2.shape[0]
    tm, tk = MLP_TM, MLP_TK
    return pl.pallas_call(
        _mlp_kernel,
        grid=(tokens // tm, D_FF // tk),
        in_specs=[
            pl.BlockSpec((tm, D_MODEL), lambda i, k: (i, 0)),
            pl.BlockSpec((D_MODEL, tk), lambda i, k: (0, k)),
            pl.BlockSpec((tk, D_MODEL), lambda i, k: (k, 0)),
        ],
        out_specs=pl.BlockSpec((tm, D_MODEL), lambda i, k: (i, 0)),
        out_shape=jax.ShapeDtypeStruct((tokens, D_MODEL), F32),
        compiler_params=_params("parallel", "arbitrary"),
        name="mlp",
    )(h2, w1, w2)


PLE_TM = 256


def _ple_kernel(x1_ref, f_ref, p_ref, wg_ref, wp_ref, lnpost_ref, lnpre_ref, lnple_ref, o_ref):
    x2 = x1_ref[...] + _rms(f_ref[...], lnpost_ref[...])
    hg = _rms(x2, lnpre_ref[...]).astype(BF16)
    gate = jax.nn.sigmoid(jnp.dot(hg, wg_ref[...], preferred_element_type=F32))
    e = jnp.dot(p_ref[...].astype(BF16), wp_ref[...], preferred_element_type=F32) * gate
    o_ref[...] = x2 + _rms(e, lnple_ref[...])


def _ple(x1, f, p2d, w_gate, w_proj, ln_mlp_post, ln_ple_pre, ln_ple_post):
    tokens = x1.shape[0]
    tm = PLE_TM
    row = lambda n: pl.BlockSpec((tm, n), lambda i: (i, 0))
    return pl.pallas_call(
        _ple_kernel,
        grid=(tokens // tm,),
        in_specs=[
            row(D_MODEL), row(D_MODEL), row(PLE_DIM),
            _resident((D_MODEL, D_MODEL)), _resident((PLE_DIM, D_MODEL)),
            _resident((1, D_MODEL)), _resident((1, D_MODEL)), _resident((1, D_MODEL)),
        ],
        out_specs=row(D_MODEL),
        out_shape=jax.ShapeDtypeStruct((tokens, D_MODEL), F32),
        compiler_params=_params("parallel"),
        name="ple",
    )(x1, f, p2d, w_gate, w_proj, ln_mlp_post, ln_ple_pre, ln_ple_post)


def _rope_tables(seq_len):
    half = HEAD_DIM // 2
    quarter = half // 2
    t = jnp.arange(seq_len)
    freqs = ROPE_THETA ** (-jnp.arange(0, half, 2, dtype=F32) / half)

    def axis_tables(pos):
        ang = pos.astype(F32)[:, None] * freqs[None, :]
        cos, sin = jnp.cos(ang), jnp.sin(ang)
        zero = jnp.zeros_like(sin)
        return (jnp.concatenate([cos, cos], -1), jnp.concatenate([-sin, zero], -1),
                jnp.concatenate([zero, sin], -1))

    row_t = axis_tables(t // GRID_W)
    col_t = axis_tables(t % GRID_W)
    assert quarter * 4 == HEAD_DIM
    return tuple(jnp.concatenate([a, b], -1) for a, b in zip(row_t, col_t))


def _trunk(x, p, weights, bias_tables, rope):
    batch, seq_len, _ = x.shape
    tokens = batch * seq_len
    x2d = x.reshape(tokens, D_MODEL)
    for i, w in enumerate(weights):
        na, gq, kv, gates = _in_projection(x2d, seq_len, w["ln_mix_pre"], w["w_in"], w["q_norm"],
                                           w["k_norm"], *rope)
        a = _neighbourhood_attention(na.reshape(batch, seq_len, 3 * NA_WIDTH), bias_tables[i])
        b = _gqa_attention(gq.reshape(batch, seq_len, GQA_WIDTH),
                           kv.reshape(batch, seq_len, 2 * KV_WIDTH))
        x1, h2 = _merge(a.reshape(tokens, NA_WIDTH), b.reshape(tokens, GQA_WIDTH), gates, x2d,
                        w["w_na_branch"], w["w_gqa_branch"], w["w_mix_out"], w["ln_mix_post"],
                        w["ln_mlp_pre"])
        f = _mlp(h2, w["w_ff1"], w["w_ff2"])
        x2d = _ple(x1, f, p[i].reshape(tokens, PLE_DIM), w["w_ple_gate"], w["w_ple_proj"],
                   w["ln_mlp_post"], w["ln_ple_pre"], w["ln_ple_post"])
    return x2d.reshape(batch, seq_len, D_MODEL)


_MATRICES = ("w_in", "w_na_branch", "w_gqa_branch", "w_mix_out", "w_ff1", "w_ff2", "w_ple_gate",
             "w_ple_proj")
_VECTORS = ("ln_mix_pre", "q_norm", "k_norm", "ln_mix_post", "ln_mlp_pre", "ln_mlp_post",
            "ln_ple_pre", "ln_ple_post")


def kernel(x_prompt, x_sample, p_prompt, p_sample, ln_mix_pre, w_in, q_norm, k_norm, na_rpb, w_na_branch, w_gqa_branch, w_mix_out, ln_mix_post, ln_mlp_pre, w_ff1, w_ff2, ln_mlp_post, ln_ple_pre, w_ple_gate, w_ple_proj, ln_ple_post):
    named = dict(ln_mix_pre=ln_mix_pre, w_in=w_in, q_norm=q_norm, k_norm=k_norm,
                 w_na_branch=w_na_branch, w_gqa_branch=w_gqa_branch, w_mix_out=w_mix_out,
                 ln_mix_post=ln_mix_post, ln_mlp_pre=ln_mlp_pre, w_ff1=w_ff1, w_ff2=w_ff2,
                 ln_mlp_post=ln_mlp_post, ln_ple_pre=ln_ple_pre, w_ple_gate=w_ple_gate,
                 w_ple_proj=w_ple_proj, ln_ple_post=ln_ple_post)
    depth = w_in.shape[0]
    weights = []
    for i in range(depth):
        layer = {n: named[n][i].astype(BF16) for n in _MATRICES}
        layer.update({n: named[n][i].reshape(1, -1) for n in _VECTORS})
        weights.append(layer)
    bias_tables = [_na_bias_table(na_rpb[i]) for i in range(depth)]
    outs = []
    for x, p in ((x_prompt, p_prompt), (x_sample, p_sample)):
        rope = _rope_tables(x.shape[1])
        outs.append(_trunk(x, p, weights, bias_tables, rope))
    return tuple(outs)
```

```python
import functools

import jax
import jax.numpy as jnp
import numpy as np
from jax import lax
from jax.experimental import pallas as pl
from jax.experimental.pallas import tpu as pltpu

D_MODEL = 2048
GRID_W = 64
HEAD_DIM = 128
NA_HEADS = 8
GQA_HEADS = 8
GQA_KV_HEADS = 2
GQA_GROUP = GQA_HEADS // GQA_KV_HEADS
NA_WIDTH = NA_HEADS * HEAD_DIM
GQA_WIDTH = GQA_HEADS * HEAD_DIM
KV_WIDTH = GQA_KV_HEADS * HEAD_DIM
NA_ROWS_MAX = 8
NA_COLS = 16
D_FF = 4 * D_MODEL
PLE_DIM = 256
ROPE_THETA = 10000.0
EPS = 1e-6
NEG_INF = -1e30
ATTN_SCALE = HEAD_DIM ** -0.5
LOG2E = 1.4426950408889634
QUERY_SCALE = ATTN_SCALE * LOG2E

RPB_ROWS = 2 * NA_ROWS_MAX - 1
RPB_COLS = 2 * NA_COLS - 1

VMEM_LIMIT_BYTES = 56 * 1024 * 1024

BF16 = jnp.bfloat16
F32 = jnp.float32


def _params(*semantics):
    return pltpu.CompilerParams(dimension_semantics=semantics, vmem_limit_bytes=VMEM_LIMIT_BYTES)


def _resident(shape):
    zeros = (0,) * len(shape)
    return pl.BlockSpec(shape, lambda *_: zeros, pipeline_mode=pl.Buffered(1))


def _rms(x, gain):
    ms = jnp.mean(x * x, axis=-1, keepdims=True)
    return x * lax.rsqrt(ms + EPS) * gain


NA_QROWS = 4
NA_KROWS = NA_QROWS + NA_ROWS_MAX
NA_BLOCK_TYPES = 3
NA_BQ = NA_QROWS * GRID_W
NA_BK = NA_KROWS * GRID_W


def _na_block_rows(block_type, rr):
    if block_type == 0:
        return 0, -rr
    if block_type == 1:
        return rr, -(NA_ROWS_MAX // 2) - rr
    return NA_QROWS, -NA_ROWS_MAX - rr


def _na_bias_kernel(rpb_ref, o_ref):
    h = pl.program_id(0)
    qc = lax.broadcasted_iota(jnp.int32, (GRID_W, 2 * GRID_W), 0)
    lane = lax.broadcasted_iota(jnp.int32, (GRID_W, 2 * GRID_W), 1)
    upper = lane >= GRID_W
    kc = jnp.where(upper, lane - GRID_W, lane)
    dc = jnp.clip(kc - qc, -(NA_COLS - 1), NA_COLS - 1) + (NA_COLS - 1)
    col_start = jnp.clip(qc - NA_COLS // 2, 0, GRID_W - NA_COLS)
    col_ok = (kc >= col_start) & (kc < col_start + NA_COLS)
    base = h * (RPB_ROWS * RPB_COLS)
    neg = jnp.full((GRID_W, 2 * GRID_W), NEG_INF, F32)
    pairs = []
    for t in range(RPB_ROWS - 1):
        acc = jnp.zeros((GRID_W, 2 * GRID_W), F32)
        for d in range(RPB_COLS):
            lo = rpb_ref[base + t * RPB_COLS + d] * LOG2E
            hi = rpb_ref[base + (t + 1) * RPB_COLS + d] * LOG2E
            acc = jnp.where(dc == d, jnp.where(upper, hi, lo), acc)
        pairs.append(jnp.where(col_ok, acc, NEG_INF))
    for block_type in range(NA_BLOCK_TYPES):
        for rr in range(NA_QROWS):
            j0, dr0 = _na_block_rows(block_type, rr)
            for jp in range(NA_KROWS // 2):
                lo_ok = j0 <= 2 * jp < j0 + NA_ROWS_MAX
                hi_ok = j0 <= 2 * jp + 1 < j0 + NA_ROWS_MAX
                if not (lo_ok or hi_ok):
                    tile = neg
                else:
                    tile = pairs[dr0 + 2 * jp + (NA_ROWS_MAX - 1)]
                    if not lo_ok:
                        tile = jnp.where(upper, tile, NEG_INF)
                    if not hi_ok:
                        tile = jnp.where(upper, NEG_INF, tile)
                o_ref[0, block_type, rr * GRID_W:(rr + 1) * GRID_W,
                      jp * 2 * GRID_W:(jp + 1) * 2 * GRID_W] = tile


def _na_bias_table(rpb):
    shape = (NA_HEADS, NA_BLOCK_TYPES, NA_BQ, NA_BK)
    return pl.pallas_call(
        _na_bias_kernel,
        grid=(NA_HEADS,),
        in_specs=[pl.BlockSpec(memory_space=pltpu.SMEM)],
        out_specs=pl.BlockSpec((1,) + shape[1:], lambda h: (h, 0, 0, 0)),
        out_shape=jax.ShapeDtypeStruct(shape, F32),
        compiler_params=_params("arbitrary"),
        name="na_bias_table",
    )(rpb.reshape(-1))


INPROJ_TM = 1024
INPROJ_TN = 512
INPROJ_CHUNK = 256


def _row_chunks():
    return [slice(r, r + INPROJ_CHUNK) for r in range(0, INPROJ_TM, INPROJ_CHUNK)]
_NA_TILES = 3 * NA_WIDTH // INPROJ_TN
_NAQ_TILES = NA_WIDTH // INPROJ_TN
_GQ_TILES = GQA_WIDTH // INPROJ_TN
_GATE_TILES = 2 * D_MODEL // INPROJ_TN
_KV_TILE = _NA_TILES + _GQ_TILES
_GATE_TILE0 = _KV_TILE + 1
assert 2 * KV_WIDTH == INPROJ_TN
_HEADS_PER_TILE = INPROJ_TN // HEAD_DIM


def _rope(y, cos, sin_lo, sin_hi):
    return y * cos + pltpu.roll(y, 32, 1) * sin_hi + pltpu.roll(y, HEAD_DIM - 32, 1) * sin_lo


def _inproj_kernel(x_ref, g_ref, w_ref, qn_ref, kn_ref, cos_ref, slo_ref, shi_ref,
                   na_ref, gq_ref, kv_ref, gt_ref, h_ref):
    j = pl.program_id(1)

    def project(epilogue, normalise=False):
        for rows in _row_chunks():
            if normalise:
                h_ref[rows, :] = _rms(x_ref[rows, :], g_ref[...]).astype(BF16)
            epilogue(rows, jnp.dot(h_ref[rows, :], w_ref[...], preferred_element_type=F32))

    def norm_rope(rows, acc, hh, gain_ref):
        sl = slice(hh * HEAD_DIM, (hh + 1) * HEAD_DIM)
        return _rope(_rms(acc[:, sl], gain_ref[...]), cos_ref[rows, :], slo_ref[rows, :],
                     shi_ref[rows, :])

    def na_q_epilogue(rows, acc):
        na_ref[rows, :] = (acc * QUERY_SCALE).astype(BF16)

    @pl.when(j == 0)
    def _():
        project(na_q_epilogue, normalise=True)

    @pl.when((j > 0) & (j < _NAQ_TILES))
    def _():
        project(na_q_epilogue)

    @pl.when((j >= _NAQ_TILES) & (j < _NA_TILES))
    def _():
        def epilogue(rows, acc):
            na_ref[rows, :] = acc.astype(BF16)
        project(epilogue)

    @pl.when((j >= _NA_TILES) & (j < _KV_TILE))
    def _():
        def epilogue(rows, acc):
            for hh in range(_HEADS_PER_TILE):
                y = norm_rope(rows, acc, hh, qn_ref)
                gq_ref[rows, hh * HEAD_DIM:(hh + 1) * HEAD_DIM] = (y * QUERY_SCALE).astype(BF16)
        project(epilogue)

    @pl.when(j == _KV_TILE)
    def _():
        def epilogue(rows, acc):
            for hh in range(GQA_KV_HEADS):
                y = norm_rope(rows, acc, hh, kn_ref)
                kv_ref[rows, hh * HEAD_DIM:(hh + 1) * HEAD_DIM] = y.astype(BF16)
            kv_ref[rows, KV_WIDTH:] = acc[:, KV_WIDTH:].astype(BF16)
        project(epilogue)

    @pl.when(j >= _GATE_TILE0)
    def _():
        def epilogue(rows, acc):
            gt_ref[rows, :] = jax.nn.sigmoid(acc).astype(BF16)
        project(epilogue)


def _in_projection(x2d, seq_len, ln_pre, w_in, q_norm, k_norm, cos, sin_lo, sin_hi):
    tokens = x2d.shape[0]
    tm, tn = INPROJ_TM, INPROJ_TN
    n_col = w_in.shape[1] // tn
    seq_tiles = seq_len // tm
    pos_spec = pl.BlockSpec((tm, HEAD_DIM), lambda i, j: (i % seq_tiles, 0))
    vec = lambda n: pl.BlockSpec((1, n), lambda i, j: (0, 0))
    return pl.pallas_call(
        _inproj_kernel,
        grid=(tokens // tm, n_col),
        in_specs=[
            pl.BlockSpec((tm, D_MODEL), lambda i, j: (i, 0)),
            vec(D_MODEL),
            pl.BlockSpec((D_MODEL, tn), lambda i, j: (0, j)),
            vec(HEAD_DIM), vec(HEAD_DIM),
            pos_spec, pos_spec, pos_spec,
        ],
        out_specs=[
            pl.BlockSpec((tm, tn), lambda i, j: (i, jnp.minimum(j, _NA_TILES - 1))),
            pl.BlockSpec((tm, tn), lambda i, j: (i, jnp.clip(j - _NA_TILES, 0, _GQ_TILES - 1))),
            pl.BlockSpec((tm, tn), lambda i, j: (i, 0)),
            pl.BlockSpec((tm, tn), lambda i, j: (i, jnp.clip(j - _GATE_TILE0, 0, _GATE_TILES - 1))),
        ],
        out_shape=[
            jax.ShapeDtypeStruct((tokens, 3 * NA_WIDTH), BF16),
            jax.ShapeDtypeStruct((tokens, GQA_WIDTH), BF16),
            jax.ShapeDtypeStruct((tokens, 2 * KV_WIDTH), BF16),
            jax.ShapeDtypeStruct((tokens, 2 * D_MODEL), BF16),
        ],
        scratch_shapes=[pltpu.VMEM((tm, D_MODEL), BF16)],
        compiler_params=_params("parallel", "arbitrary"),
        name="in_projection",
    )(x2d, ln_pre, w_in, q_norm, k_norm, cos, sin_lo, sin_hi)


def _na_kernel(q_ref, k_ref, v_ref, bias_ref, o_ref, *, rows):
    n_blocks = rows // NA_QROWS

    def body(blk, carry):
        key_row0 = jnp.clip(NA_QROWS * blk - NA_ROWS_MAX // 2, 0, rows - NA_KROWS)
        block_type = jnp.where(blk == 0, 0, jnp.where(blk == n_blocks - 1, 2, 1))
        qoff = pl.multiple_of(blk * NA_BQ, NA_BQ)
        koff = pl.multiple_of(key_row0 * GRID_W, GRID_W)
        q = q_ref[0, pl.ds(qoff, NA_BQ), :]
        k = k_ref[0, pl.ds(koff, NA_BK), :]
        v = v_ref[0, pl.ds(koff, NA_BK), :]
        s = lax.dot_general(q, k, (((1,), (1,)), ((), ())), preferred_element_type=F32)
        s = s + bias_ref[0, block_type]
        m = jnp.max(s, axis=-1, keepdims=True)
        p = jnp.exp2(s - m)
        l = jnp.sum(p, axis=-1, keepdims=True)
        o = jnp.dot(p.astype(BF16), v, preferred_element_type=F32)
        o_ref[0, pl.ds(qoff, NA_BQ), :] = (o / l).astype(BF16)
        return carry

    lax.fori_loop(0, n_blocks, body, 0, unroll=4)


def _neighbourhood_attention(na3d, bias):
    batch, seq_len, _ = na3d.shape
    rows = seq_len // GRID_W
    assert rows % NA_QROWS == 0 and rows >= NA_KROWS + NA_QROWS
    head_spec = lambda off: pl.BlockSpec((1, seq_len, HEAD_DIM), lambda h, b: (b, 0, h + off))
    return pl.pallas_call(
        functools.partial(_na_kernel, rows=rows),
        grid=(NA_HEADS, batch),
        in_specs=[
            head_spec(0), head_spec(NA_HEADS), head_spec(2 * NA_HEADS),
            pl.BlockSpec((1, NA_BLOCK_TYPES, NA_BQ, NA_BK), lambda h, b: (h, 0, 0, 0)),
        ],
        out_specs=pl.BlockSpec((1, seq_len, HEAD_DIM), lambda h, b: (b, 0, h)),
        out_shape=jax.ShapeDtypeStruct((batch, seq_len, NA_WIDTH), BF16),
        compiler_params=_params("parallel", "parallel"),
        name="neighbourhood_attention",
    )(na3d, na3d, na3d, bias)


GQA_TQ = 512


GQA_NK = 256


def _gqa_kernel(q_ref, k_ref, v_ref, o_ref, vext_ref):
    seq_len = k_ref.shape[1]

    @pl.when(pl.program_id(2) == 0)
    def _():
        vext_ref[:, :HEAD_DIM] = v_ref[0]
        vext_ref[:, HEAD_DIM:] = jnp.ones((seq_len, HEAD_DIM), BF16)

    tq = q_ref.shape[1]
    heads = [slice(g * HEAD_DIM, (g + 1) * HEAD_DIM) for g in range(GQA_GROUP)]
    q = jnp.concatenate([q_ref[0, :, sl] for sl in heads], axis=0)
    m = acc = None
    for c in range(seq_len // GQA_NK):
        keys = slice(c * GQA_NK, (c + 1) * GQA_NK)
        s = lax.dot_general(q, k_ref[0, keys, :], (((1,), (1,)), ((), ())),
                            preferred_element_type=F32)
        blocks = [s[:, n * HEAD_DIM:(n + 1) * HEAD_DIM] for n in range(GQA_NK // HEAD_DIM)]
        block_max = functools.reduce(jnp.maximum, blocks)
        row_max = jnp.broadcast_to(jnp.max(block_max, axis=-1, keepdims=True), block_max.shape)
        m_new = row_max if m is None else jnp.maximum(m, row_max)
        p = jnp.concatenate([jnp.exp2(b - m_new) for b in blocks], axis=1).astype(BF16)
        pv = jnp.dot(p, vext_ref[keys, :], preferred_element_type=F32)
        if acc is None:
            acc = pv
        else:
            alpha = jnp.exp2(m - m_new)
            acc = jnp.concatenate([acc[:, :HEAD_DIM] * alpha, acc[:, HEAD_DIM:] * alpha], axis=1) + pv
        m = m_new
    o = acc[:, :HEAD_DIM] / acc[:, HEAD_DIM:]
    for g, sl in enumerate(heads):
        o_ref[0, :, sl] = o[g * tq:(g + 1) * tq].astype(BF16)


def _gqa_attention(q3d, kv3d):
    batch, seq_len, _ = q3d.shape
    tq = GQA_TQ
    group_w = GQA_GROUP * HEAD_DIM
    return pl.pallas_call(
        _gqa_kernel,
        grid=(batch, GQA_KV_HEADS, seq_len // tq),
        in_specs=[
            pl.BlockSpec((1, tq, group_w), lambda b, kv, qi: (b, qi, kv)),
            pl.BlockSpec((1, seq_len, HEAD_DIM), lambda b, kv, qi: (b, 0, kv)),
            pl.BlockSpec((1, seq_len, HEAD_DIM), lambda b, kv, qi: (b, 0, GQA_KV_HEADS + kv)),
        ],
        out_specs=pl.BlockSpec((1, tq, group_w), lambda b, kv, qi: (b, qi, kv)),
        out_shape=jax.ShapeDtypeStruct((batch, seq_len, GQA_WIDTH), BF16),
        scratch_shapes=[pltpu.VMEM((seq_len, 2 * HEAD_DIM), BF16)],
        compiler_params=_params("parallel", "parallel", "arbitrary"),
        name="gqa_attention",
    )(q3d, kv3d, kv3d)


MERGE_TM = 512
MERGE_CHUNK = 256


def _merge_kernel(a_ref, b_ref, ga_ref, gb_ref, x_ref, wna_ref, wgq_ref, wmix_ref,
                  lnpost_ref, lnmlp_ref, x1_ref, h2_ref):
    for r in range(0, MERGE_TM, MERGE_CHUNK):
        rows = slice(r, r + MERGE_CHUNK)
        pa = jnp.dot(a_ref[rows, :], wna_ref[...], preferred_element_type=F32)
        pb = jnp.dot(b_ref[rows, :], wgq_ref[...], preferred_element_type=F32)
        merged = ga_ref[rows, :].astype(F32) * pa + gb_ref[rows, :].astype(F32) * pb
        y = jnp.dot(merged.astype(BF16), wmix_ref[...], preferred_element_type=F32)
        x1 = x_ref[rows, :] + _rms(y, lnpost_ref[...])
        x1_ref[rows, :] = x1
        h2_ref[rows, :] = _rms(x1, lnmlp_ref[...]).astype(BF16)


def _merge(a, b, gates, x2d, w_na, w_gq, w_mix, ln_post, ln_mlp):
    tokens = x2d.shape[0]
    tm = MERGE_TM
    row = lambda n, c=0: pl.BlockSpec((tm, n), lambda i: (i, c))
    return pl.pallas_call(
        _merge_kernel,
        grid=(tokens // tm,),
        in_specs=[
            row(NA_WIDTH), row(GQA_WIDTH), row(D_MODEL, 0), row(D_MODEL, 1), row(D_MODEL),
            _resident((NA_WIDTH, D_MODEL)), _resident((GQA_WIDTH, D_MODEL)),
            _resident((D_MODEL, D_MODEL)), _resident((1, D_MODEL)), _resident((1, D_MODEL)),
        ],
        out_specs=[row(D_MODEL), row(D_MODEL)],
        out_shape=[jax.ShapeDtypeStruct((tokens, D_MODEL), F32),
                   jax.ShapeDtypeStruct((tokens, D_MODEL), BF16)],
        compiler_params=_params("parallel"),
        name="branch_merge",
    )(a, b, gates, gates, x2d, w_na, w_gq, w_mix, ln_post, ln_mlp)


MLP_TM = 1024
MLP_TK = 1024


MLP_CHUNK = 256


def _mlp_kernel(h_ref, w1_ref, w2_ref, f_ref):
    k = pl.program_id(1)

    def step(accumulate):
        for r in range(0, MLP_TM, MLP_CHUNK):
            rows = slice(r, r + MLP_CHUNK)
            u = jnp.maximum(jnp.dot(h_ref[rows, :], w1_ref[...], preferred_element_type=F32), 0.0)
            contrib = jnp.dot((u * u).astype(BF16), w2_ref[...], preferred_element_type=F32)
            if accumulate:
                f_ref[rows, :] += contrib
            else:
                f_ref[rows, :] = contrib

    @pl.when(k == 0)
    def _():
        step(False)

    @pl.when(k > 0)
    def _():
        step(True)


def _mlp(h2, w1, w2):
    tokens = h2.shape[0]
    tm, tk = MLP_TM, MLP_TK
    return pl.pallas_call(
        _mlp_kernel,
        grid=(tokens // tm, D_FF // tk),
        in_specs=[
            pl.BlockSpec((tm, D_MODEL), lambda i, k: (i, 0)),
            pl.BlockSpec((D_MODEL, tk), lambda i, k: (0, k)),
            pl.BlockSpec((tk, D_MODEL), lambda i, k: (k, 0)),
        ],
        out_specs=pl.BlockSpec((tm, D_MODEL), lambda i, k: (i, 0)),
        out_shape=jax.ShapeDtypeStruct((tokens, D_MODEL), F32),
        compiler_params=_params("parallel", "arbitrary"),
        name="mlp",
    )(h2, w1, w2)


PLE_TM = 512
PLE_CHUNK = 256


def _ple_kernel(x1_ref, f_ref, p_ref, wg_ref, wp_ref, lnpost_ref, lnpre_ref, lnple_ref, o_ref):
    for r in range(0, PLE_TM, PLE_CHUNK):
        rows = slice(r, r + PLE_CHUNK)
        x2 = x1_ref[rows, :] + _rms(f_ref[rows, :], lnpost_ref[...])
        hg = _rms(x2, lnpre_ref[...]).astype(BF16)
        gate = jax.nn.sigmoid(jnp.dot(hg, wg_ref[...], preferred_element_type=F32))
        e = jnp.dot(p_ref[rows, :].astype(BF16), wp_ref[...], preferred_element_type=F32) * gate
        o_ref[rows, :] = x2 + _rms(e, lnple_ref[...])


def _ple(x1, f, p2d, w_gate, w_proj, ln_mlp_post, ln_ple_pre, ln_ple_post):
    tokens = x1.shape[0]
    tm = PLE_TM
    row = lambda n: pl.BlockSpec((tm, n), lambda i: (i, 0))
    return pl.pallas_call(
        _ple_kernel,
        grid=(tokens // tm,),
        in_specs=[
            row(D_MODEL), row(D_MODEL), row(PLE_DIM),
            _resident((D_MODEL, D_MODEL)), _resident((PLE_DIM, D_MODEL)),
            _resident((1, D_MODEL)), _resident((1, D_MODEL)), _resident((1, D_MODEL)),
        ],
        out_specs=row(D_MODEL),
        out_shape=jax.ShapeDtypeStruct((tokens, D_MODEL), F32),
        compiler_params=_params("parallel"),
        name="ple",
    )(x1, f, p2d, w_gate, w_proj, ln_mlp_post, ln_ple_pre, ln_ple_post)


def _rope_tables(seq_len):
    half = HEAD_DIM // 2
    quarter = half // 2
    t = jnp.arange(seq_len)
    freqs = ROPE_THETA ** (-jnp.arange(0, half, 2, dtype=F32) / half)

    def axis_tables(pos):
        ang = pos.astype(F32)[:, None] * freqs[None, :]
        cos, sin = jnp.cos(ang), jnp.sin(ang)
        zero = jnp.zeros_like(sin)
        return (jnp.concatenate([cos, cos], -1), jnp.concatenate([-sin, zero], -1),
                jnp.concatenate([zero, sin], -1))

    row_t = axis_tables(t // GRID_W)
    col_t = axis_tables(t % GRID_W)
    assert quarter * 4 == HEAD_DIM
    return tuple(jnp.concatenate([a, b], -1) for a, b in zip(row_t, col_t))


def _trunk(x, p, weights, bias_tables, rope):
    batch, seq_len, _ = x.shape
    tokens = batch * seq_len
    x2d = x.reshape(tokens, D_MODEL)
    for i, w in enumerate(weights):
        na, gq, kv, gates = _in_projection(x2d, seq_len, w["ln_mix_pre"], w["w_in"], w["q_norm"],
                                           w["k_norm"], *rope)
        a = _neighbourhood_attention(na.reshape(batch, seq_len, 3 * NA_WIDTH), bias_tables[i])
        b = _gqa_attention(gq.reshape(batch, seq_len, GQA_WIDTH),
                           kv.reshape(batch, seq_len, 2 * KV_WIDTH))
        x1, h2 = _merge(a.reshape(tokens, NA_WIDTH), b.reshape(tokens, GQA_WIDTH), gates, x2d,
                        w["w_na_branch"], w["w_gqa_branch"], w["w_mix_out"], w["ln_mix_post"],
                        w["ln_mlp_pre"])
        f = _mlp(h2, w["w_ff1"], w["w_ff2"])
        x2d = _ple(x1, f, p[i].reshape(tokens, PLE_DIM), w["w_ple_gate"], w["w_ple_proj"],
                   w["ln_mlp_post"], w["ln_ple_pre"], w["ln_ple_post"])
    return x2d.reshape(batch, seq_len, D_MODEL)


_MATRICES = ("w_in", "w_na_branch", "w_gqa_branch", "w_mix_out", "w_ff1", "w_ff2", "w_ple_gate",
             "w_ple_proj")
_VECTORS = ("ln_mix_pre", "q_norm", "k_norm", "ln_mix_post", "ln_mlp_pre", "ln_mlp_post",
            "ln_ple_pre", "ln_ple_post")


def kernel(x_prompt, x_sample, p_prompt, p_sample, ln_mix_pre, w_in, q_norm, k_norm, na_rpb, w_na_branch, w_gqa_branch, w_mix_out, ln_mix_post, ln_mlp_pre, w_ff1, w_ff2, ln_mlp_post, ln_ple_pre, w_ple_gate, w_ple_proj, ln_ple_post):
    named = dict(ln_mix_pre=ln_mix_pre, w_in=w_in, q_norm=q_norm, k_norm=k_norm,
                 w_na_branch=w_na_branch, w_gqa_branch=w_gqa_branch, w_mix_out=w_mix_out,
                 ln_mix_post=ln_mix_post, ln_mlp_pre=ln_mlp_pre, w_ff1=w_ff1, w_ff2=w_ff2,
                 ln_mlp_post=ln_mlp_post, ln_ple_pre=ln_ple_pre, w_ple_gate=w_ple_gate,
                 w_ple_proj=w_ple_proj, ln_ple_post=ln_ple_post)
    depth = w_in.shape[0]
    weights = []
    for i in range(depth):
        layer = {n: named[n][i].astype(BF16) for n in _MATRICES}
        layer.update({n: named[n][i].reshape(1, -1) for n in _VECTORS})
        weights.append(layer)
    bias_tables = [_na_bias_table(na_rpb[i]) for i in range(depth)]
    outs = []
    for x, p in ((x_prompt, p_prompt), (x_sample, p_sample)):
        rope = _rope_tables(x.shape[1])
        outs.append(_trunk(x, p, weights, bias_tables, rope))
    return tuple(outs)
```

```python
import functools

import jax
import jax.numpy as jnp
import numpy as np
from jax import lax
from jax.experimental import pallas as pl
from jax.experimental.pallas import tpu as pltpu

D_MODEL = 2048
GRID_W = 64
HEAD_DIM = 128
NA_HEADS = 8
GQA_HEADS = 8
GQA_KV_HEADS = 2
GQA_GROUP = GQA_HEADS // GQA_KV_HEADS
NA_WIDTH = NA_HEADS * HEAD_DIM
GQA_WIDTH = GQA_HEADS * HEAD_DIM
KV_WIDTH = GQA_KV_HEADS * HEAD_DIM
NA_ROWS_MAX = 8
NA_COLS = 16
D_FF = 4 * D_MODEL
PLE_DIM = 256
ROPE_THETA = 10000.0
EPS = 1e-6
NEG_INF = -1e30
ATTN_SCALE = HEAD_DIM ** -0.5
LOG2E = 1.4426950408889634
QUERY_SCALE = ATTN_SCALE * LOG2E

RPB_ROWS = 2 * NA_ROWS_MAX - 1
RPB_COLS = 2 * NA_COLS - 1

VMEM_LIMIT_BYTES = 56 * 1024 * 1024

BF16 = jnp.bfloat16
F32 = jnp.float32


def _params(*semantics):
    return pltpu.CompilerParams(dimension_semantics=semantics, vmem_limit_bytes=VMEM_LIMIT_BYTES)


def _resident(shape):
    zeros = (0,) * len(shape)
    return pl.BlockSpec(shape, lambda *_: zeros, pipeline_mode=pl.Buffered(1))


def _rms(x, gain):
    ms = jnp.mean(x * x, axis=-1, keepdims=True)
    return x * lax.rsqrt(ms + EPS) * gain


NA_QROWS = 4
NA_KROWS = NA_QROWS + NA_ROWS_MAX
NA_BLOCK_TYPES = NA_ROWS_MAX // NA_QROWS + 1
NA_BQ = NA_QROWS * GRID_W
NA_BK = NA_KROWS * GRID_W
assert (NA_ROWS_MAX // 2) % NA_QROWS == 0 and NA_KROWS % 2 == 0


def _na_block_rows(block_type, rr):
    offset = block_type * NA_QROWS
    if offset < NA_ROWS_MAX // 2:
        j0 = 0
    elif offset == NA_ROWS_MAX // 2:
        j0 = rr
    else:
        j0 = NA_KROWS - NA_ROWS_MAX
    return j0, -offset - rr


def _na_bias_kernel(rpb_ref, o_ref):
    h = pl.program_id(0)
    qc = lax.broadcasted_iota(jnp.int32, (GRID_W, 2 * GRID_W), 0)
    lane = lax.broadcasted_iota(jnp.int32, (GRID_W, 2 * GRID_W), 1)
    upper = lane >= GRID_W
    kc = jnp.where(upper, lane - GRID_W, lane)
    dc = jnp.clip(kc - qc, -(NA_COLS - 1), NA_COLS - 1) + (NA_COLS - 1)
    col_start = jnp.clip(qc - NA_COLS // 2, 0, GRID_W - NA_COLS)
    col_ok = (kc >= col_start) & (kc < col_start + NA_COLS)
    base = h * (RPB_ROWS * RPB_COLS)
    neg = jnp.full((GRID_W, 2 * GRID_W), NEG_INF, F32)
    pairs = []
    for t in range(RPB_ROWS - 1):
        acc = jnp.zeros((GRID_W, 2 * GRID_W), F32)
        for d in range(RPB_COLS):
            lo = rpb_ref[base + t * RPB_COLS + d] * LOG2E
            hi = rpb_ref[base + (t + 1) * RPB_COLS + d] * LOG2E
            acc = jnp.where(dc == d, jnp.where(upper, hi, lo), acc)
        pairs.append(jnp.where(col_ok, acc, NEG_INF))
    for block_type in range(NA_BLOCK_TYPES):
        for rr in range(NA_QROWS):
            j0, dr0 = _na_block_rows(block_type, rr)
            for jp in range(NA_KROWS // 2):
                lo_ok = j0 <= 2 * jp < j0 + NA_ROWS_MAX
                hi_ok = j0 <= 2 * jp + 1 < j0 + NA_ROWS_MAX
                if not (lo_ok or hi_ok):
                    tile = neg
                else:
                    tile = pairs[dr0 + 2 * jp + (NA_ROWS_MAX - 1)]
                    if not lo_ok:
                        tile = jnp.where(upper, tile, NEG_INF)
                    if not hi_ok:
                        tile = jnp.where(upper, NEG_INF, tile)
                o_ref[0, block_type, rr * GRID_W:(rr + 1) * GRID_W,
                      jp * 2 * GRID_W:(jp + 1) * 2 * GRID_W] = tile


def _na_bias_table(rpb):
    shape = (NA_HEADS, NA_BLOCK_TYPES, NA_BQ, NA_BK)
    return pl.pallas_call(
        _na_bias_kernel,
        grid=(NA_HEADS,),
        in_specs=[pl.BlockSpec(memory_space=pltpu.SMEM)],
        out_specs=pl.BlockSpec((1,) + shape[1:], lambda h: (h, 0, 0, 0)),
        out_shape=jax.ShapeDtypeStruct(shape, F32),
        compiler_params=_params("arbitrary"),
        name="na_bias_table",
    )(rpb.reshape(-1))


INPROJ_TM = 256
INPROJ_TN = 512
_COL_GQ = 3 * NA_WIDTH
_COL_GK = _COL_GQ + GQA_WIDTH
_COL_GV = _COL_GK + KV_WIDTH
_COL_GATES = _COL_GV + KV_WIDTH
IN_COLS = _COL_GATES + 2 * D_MODEL


def _rope(y, cos, sin_lo, sin_hi):
    return y * cos + pltpu.roll(y, 32, 1) * sin_hi + pltpu.roll(y, HEAD_DIM - 32, 1) * sin_lo


def _inproj_kernel(x_ref, g_ref, w_ref, qn_ref, kn_ref, cos_ref, slo_ref, shi_ref,
                   na_ref, gq_ref, kv_ref, gt_ref, h_ref):
    h_ref[...] = _rms(x_ref[...], g_ref[...]).astype(BF16)

    def project(col, width=INPROJ_TN):
        return jnp.dot(h_ref[...], w_ref[:, col:col + width], preferred_element_type=F32)

    def norm_rope(acc, hh, gain_ref):
        sl = slice(hh * HEAD_DIM, (hh + 1) * HEAD_DIM)
        return _rope(_rms(acc[:, sl], gain_ref[...]), cos_ref[...], slo_ref[...], shi_ref[...])

    for col in range(0, 3 * NA_WIDTH, INPROJ_TN):
        acc = project(col)
        if col < NA_WIDTH:
            acc = acc * QUERY_SCALE
        na_ref[:, col:col + INPROJ_TN] = acc.astype(BF16)

    for col in range(0, GQA_WIDTH, INPROJ_TN):
        acc = project(_COL_GQ + col)
        for hh in range(INPROJ_TN // HEAD_DIM):
            out = slice(col + hh * HEAD_DIM, col + (hh + 1) * HEAD_DIM)
            gq_ref[:, out] = (norm_rope(acc, hh, qn_ref) * QUERY_SCALE).astype(BF16)

    acc = project(_COL_GK, 2 * KV_WIDTH)
    for hh in range(GQA_KV_HEADS):
        kv_ref[:, hh * HEAD_DIM:(hh + 1) * HEAD_DIM] = norm_rope(acc, hh, kn_ref).astype(BF16)
    kv_ref[:, KV_WIDTH:] = acc[:, KV_WIDTH:].astype(BF16)

    for col in range(0, 2 * D_MODEL, INPROJ_TN):
        gt_ref[:, col:col + INPROJ_TN] = jax.nn.sigmoid(project(_COL_GATES + col)).astype(BF16)


def _in_projection(x2d, seq_len, ln_pre, w_in, q_norm, k_norm, cos, sin_lo, sin_hi):
    tokens = x2d.shape[0]
    tm = INPROJ_TM
    assert w_in.shape == (D_MODEL, IN_COLS)
    seq_tiles = seq_len // tm
    row = lambda n: pl.BlockSpec((tm, n), lambda i: (i, 0))
    pos_spec = pl.BlockSpec((tm, HEAD_DIM), lambda i: (i % seq_tiles, 0))
    return pl.pallas_call(
        _inproj_kernel,
        grid=(tokens // tm,),
        in_specs=[
            row(D_MODEL), _resident((1, D_MODEL)), _resident((D_MODEL, IN_COLS)),
            _resident((1, HEAD_DIM)), _resident((1, HEAD_DIM)),
            pos_spec, pos_spec, pos_spec,
        ],
        out_specs=[row(3 * NA_WIDTH), row(GQA_WIDTH), row(2 * KV_WIDTH), row(2 * D_MODEL)],
        out_shape=[
            jax.ShapeDtypeStruct((tokens, 3 * NA_WIDTH), BF16),
            jax.ShapeDtypeStruct((tokens, GQA_WIDTH), BF16),
            jax.ShapeDtypeStruct((tokens, 2 * KV_WIDTH), BF16),
            jax.ShapeDtypeStruct((tokens, 2 * D_MODEL), BF16),
        ],
        scratch_shapes=[pltpu.VMEM((tm, D_MODEL), BF16)],
        compiler_params=_params("parallel"),
        name="in_projection",
    )(x2d, ln_pre, w_in, q_norm, k_norm, cos, sin_lo, sin_hi)


def _na_kernel(q_ref, k_ref, v_ref, bias_ref, o_ref, *, rows):
    for blk in range(rows // NA_QROWS):
        key_row0 = min(max(NA_QROWS * blk - NA_ROWS_MAX // 2, 0), rows - NA_KROWS)
        block_type = (NA_QROWS * blk - key_row0) // NA_QROWS
        queries = slice(blk * NA_BQ, (blk + 1) * NA_BQ)
        keys = slice(key_row0 * GRID_W, key_row0 * GRID_W + NA_BK)
        s = lax.dot_general(q_ref[0, queries, :], k_ref[0, keys, :], (((1,), (1,)), ((), ())),
                            preferred_element_type=F32)
        s = s + bias_ref[0, block_type]
        m = jnp.max(s, axis=-1, keepdims=True)
        p = jnp.exp2(s - m)
        l = jnp.sum(p, axis=-1, keepdims=True)
        o = jnp.dot(p.astype(BF16), v_ref[0, keys, :], preferred_element_type=F32)
        o_ref[0, queries, :] = (o / l).astype(BF16)


def _neighbourhood_attention(na3d, bias):
    batch, seq_len, _ = na3d.shape
    rows = seq_len // GRID_W
    assert rows % NA_QROWS == 0 and rows >= 2 * NA_ROWS_MAX
    head_spec = lambda off: pl.BlockSpec((1, seq_len, HEAD_DIM), lambda h, b: (b, 0, h + off))
    return pl.pallas_call(
        functools.partial(_na_kernel, rows=rows),
        grid=(NA_HEADS, batch),
        in_specs=[
            head_spec(0), head_spec(NA_HEADS), head_spec(2 * NA_HEADS),
            pl.BlockSpec((1, NA_BLOCK_TYPES, NA_BQ, NA_BK), lambda h, b: (h, 0, 0, 0)),
        ],
        out_specs=pl.BlockSpec((1, seq_len, HEAD_DIM), lambda h, b: (b, 0, h)),
        out_shape=jax.ShapeDtypeStruct((batch, seq_len, NA_WIDTH), BF16),
        compiler_params=_params("parallel", "parallel"),
        name="neighbourhood_attention",
    )(na3d, na3d, na3d, bias)


GQA_TQ = 1024


GQA_NK = 256


def _gqa_kernel(q_ref, k_ref, v_ref, o_ref, vext_ref):
    seq_len = k_ref.shape[1]

    @pl.when(pl.program_id(2) == 0)
    def _():
        vext_ref[:, :HEAD_DIM] = v_ref[0]
        vext_ref[:, HEAD_DIM:] = jnp.ones((seq_len, HEAD_DIM), BF16)

    tq = q_ref.shape[1]
    heads = [slice(g * HEAD_DIM, (g + 1) * HEAD_DIM) for g in range(GQA_GROUP)]
    q = jnp.concatenate([q_ref[0, :, sl] for sl in heads], axis=0)
    m = acc = None
    for c in range(seq_len // GQA_NK):
        keys = slice(c * GQA_NK, (c + 1) * GQA_NK)
        s = lax.dot_general(q, k_ref[0, keys, :], (((1,), (1,)), ((), ())),
                            preferred_element_type=F32)
        blocks = [s[:, n * HEAD_DIM:(n + 1) * HEAD_DIM] for n in range(GQA_NK // HEAD_DIM)]
        block_max = functools.reduce(jnp.maximum, blocks)
        row_max = jnp.broadcast_to(jnp.max(block_max, axis=-1, keepdims=True), block_max.shape)
        m_new = row_max if m is None else jnp.maximum(m, row_max)
        p = jnp.concatenate([jnp.exp2(b - m_new) for b in blocks], axis=1).astype(BF16)
        pv = jnp.dot(p, vext_ref[keys, :], preferred_element_type=F32)
        if acc is None:
            acc = pv
        else:
            alpha = jnp.exp2(m - m_new)
            acc = jnp.concatenate([acc[:, :HEAD_DIM] * alpha, acc[:, HEAD_DIM:] * alpha], axis=1) + pv
        m = m_new
    o = acc[:, :HEAD_DIM] / acc[:, HEAD_DIM:]
    for g, sl in enumerate(heads):
        o_ref[0, :, sl] = o[g * tq:(g + 1) * tq].astype(BF16)


def _gqa_attention(q3d, kv3d):
    batch, seq_len, _ = q3d.shape
    tq = GQA_TQ
    group_w = GQA_GROUP * HEAD_DIM
    return pl.pallas_call(
        _gqa_kernel,
        grid=(batch, GQA_KV_HEADS, seq_len // tq),
        in_specs=[
            pl.BlockSpec((1, tq, group_w), lambda b, kv, qi: (b, qi, kv)),
            pl.BlockSpec((1, seq_len, HEAD_DIM), lambda b, kv, qi: (b, 0, kv)),
            pl.BlockSpec((1, seq_len, HEAD_DIM), lambda b, kv, qi: (b, 0, GQA_KV_HEADS + kv)),
        ],
        out_specs=pl.BlockSpec((1, tq, group_w), lambda b, kv, qi: (b, qi, kv)),
        out_shape=jax.ShapeDtypeStruct((batch, seq_len, GQA_WIDTH), BF16),
        scratch_shapes=[pltpu.VMEM((seq_len, 2 * HEAD_DIM), BF16)],
        compiler_params=_params("parallel", "parallel", "arbitrary"),
        name="gqa_attention",
    )(q3d, kv3d, kv3d)


MERGE_TM = 512
MERGE_CHUNK = 256


def _merge_kernel(a_ref, b_ref, ga_ref, gb_ref, x_ref, wna_ref, wgq_ref, wmix_ref,
                  lnpost_ref, lnmlp_ref, x1_ref, h2_ref):
    for r in range(0, MERGE_TM, MERGE_CHUNK):
        rows = slice(r, r + MERGE_CHUNK)
        pa = jnp.dot(a_ref[rows, :], wna_ref[...], preferred_element_type=F32)
        pb = jnp.dot(b_ref[rows, :], wgq_ref[...], preferred_element_type=F32)
        merged = ga_ref[rows, :].astype(F32) * pa + gb_ref[rows, :].astype(F32) * pb
        y = jnp.dot(merged.astype(BF16), wmix_ref[...], preferred_element_type=F32)
        x1 = x_ref[rows, :] + _rms(y, lnpost_ref[...])
        x1_ref[rows, :] = x1
        h2_ref[rows, :] = _rms(x1, lnmlp_ref[...]).astype(BF16)


def _merge(a, b, gates, x2d, w_na, w_gq, w_mix, ln_post, ln_mlp):
    tokens = x2d.shape[0]
    tm = MERGE_TM
    row = lambda n, c=0: pl.BlockSpec((tm, n), lambda i: (i, c))
    return pl.pallas_call(
        _merge_kernel,
        grid=(tokens // tm,),
        in_specs=[
            row(NA_WIDTH), row(GQA_WIDTH), row(D_MODEL, 0), row(D_MODEL, 1), row(D_MODEL),
            _resident((NA_WIDTH, D_MODEL)), _resident((GQA_WIDTH, D_MODEL)),
            _resident((D_MODEL, D_MODEL)), _resident((1, D_MODEL)), _resident((1, D_MODEL)),
        ],
        out_specs=[row(D_MODEL), row(D_MODEL)],
        out_shape=[jax.ShapeDtypeStruct((tokens, D_MODEL), F32),
                   jax.ShapeDtypeStruct((tokens, D_MODEL), BF16)],
        compiler_params=_params("parallel"),
        name="branch_merge",
    )(a, b, gates, gates, x2d, w_na, w_gq, w_mix, ln_post, ln_mlp)


MLP_TM = 1024
MLP_TK = 1024


MLP_CHUNK = 256


def _mlp_kernel(h_ref, w1_ref, w2_ref, f_ref):
    k = pl.program_id(1)

    def step(accumulate):
        for r in range(0, MLP_TM, MLP_CHUNK):
            rows = slice(r, r + MLP_CHUNK)
            u = jnp.maximum(jnp.dot(h_ref[rows, :], w1_ref[...], preferred_element_type=F32), 0.0)
            contrib = jnp.dot((u * u).astype(BF16), w2_ref[...], preferred_element_type=F32)
            if accumulate:
                f_ref[rows, :] += contrib
            else:
                f_ref[rows, :] = contrib

    @pl.when(k == 0)
    def _():
        step(False)

    @pl.when(k > 0)
    def _():
        step(True)


def _mlp(h2, w1, w2):
    tokens = h2.shape[0]
    tm, tk = MLP_TM, MLP_TK
    return pl.pallas_call(
        _mlp_kernel,
        grid=(tokens // tm, D_FF // tk),
        in_specs=[
            pl.BlockSpec((tm, D_MODEL), lambda i, k: (i, 0)),
            pl.BlockSpec((D_MODEL, tk), lambda i, k: (0, k)),
            pl.BlockSpec((tk, D_MODEL), lambda i, k: (k, 0)),
        ],
        out_specs=pl.BlockSpec((tm, D_MODEL), lambda i, k: (i, 0)),
        out_shape=jax.ShapeDtypeStruct((tokens, D_MODEL), F32),
        compiler_params=_params("parallel", "arbitrary"),
        name="mlp",
    )(h2, w1, w2)


PLE_TM = 512
PLE_CHUNK = 256


def _ple_kernel(x1_ref, f_ref, p_ref, wg_ref, wp_ref, lnpost_ref, lnpre_ref, lnple_ref, o_ref):
    for r in range(0, PLE_TM, PLE_CHUNK):
        rows = slice(r, r + PLE_CHUNK)
        x2 = x1_ref[rows, :] + _rms(f_ref[rows, :], lnpost_ref[...])
        hg = _rms(x2, lnpre_ref[...]).astype(BF16)
        gate = jax.nn.sigmoid(jnp.dot(hg, wg_ref[...], preferred_element_type=F32))
        e = jnp.dot(p_ref[rows, :].astype(BF16), wp_ref[...], preferred_element_type=F32) * gate
        o_ref[rows, :] = x2 + _rms(e, lnple_ref[...])


def _ple(x1, f, p2d, w_gate, w_proj, ln_mlp_post, ln_ple_pre, ln_ple_post):
    tokens = x1.shape[0]
    tm = PLE_TM
    row = lambda n: pl.BlockSpec((tm, n), lambda i: (i, 0))
    return pl.pallas_call(
        _ple_kernel,
        grid=(tokens // tm,),
        in_specs=[
            row(D_MODEL), row(D_MODEL), row(PLE_DIM),
            _resident((D_MODEL, D_MODEL)), _resident((PLE_DIM, D_MODEL)),
            _resident((1, D_MODEL)), _resident((1, D_MODEL)), _resident((1, D_MODEL)),
        ],
        out_specs=row(D_MODEL),
        out_shape=jax.ShapeDtypeStruct((tokens, D_MODEL), F32),
        compiler_params=_params("parallel"),
        name="ple",
    )(x1, f, p2d, w_gate, w_proj, ln_mlp_post, ln_ple_pre, ln_ple_post)


def _rope_tables(seq_len):
    half = HEAD_DIM // 2
    quarter = half // 2
    t = jnp.arange(seq_len)
    freqs = ROPE_THETA ** (-jnp.arange(0, half, 2, dtype=F32) / half)

    def axis_tables(pos):
        ang = pos.astype(F32)[:, None] * freqs[None, :]
        cos, sin = jnp.cos(ang), jnp.sin(ang)
        zero = jnp.zeros_like(sin)
        return (jnp.concatenate([cos, cos], -1), jnp.concatenate([-sin, zero], -1),
                jnp.concatenate([zero, sin], -1))

    row_t = axis_tables(t // GRID_W)
    col_t = axis_tables(t % GRID_W)
    assert quarter * 4 == HEAD_DIM
    return tuple(jnp.concatenate([a, b], -1) for a, b in zip(row_t, col_t))


def _trunk(x, p, weights, bias_tables, rope):
    batch, seq_len, _ = x.shape
    tokens = batch * seq_len
    x2d = x.reshape(tokens, D_MODEL)
    for i, w in enumerate(weights):
        na, gq, kv, gates = _in_projection(x2d, seq_len, w["ln_mix_pre"], w["w_in"], w["q_norm"],
                                           w["k_norm"], *rope)
        a = _neighbourhood_attention(na.reshape(batch, seq_len, 3 * NA_WIDTH), bias_tables[i])
        b = _gqa_attention(gq.reshape(batch, seq_len, GQA_WIDTH),
                           kv.reshape(batch, seq_len, 2 * KV_WIDTH))
        x1, h2 = _merge(a.reshape(tokens, NA_WIDTH), b.reshape(tokens, GQA_WIDTH), gates, x2d,
                        w["w_na_branch"], w["w_gqa_branch"], w["w_mix_out"], w["ln_mix_post"],
                        w["ln_mlp_pre"])
        f = _mlp(h2, w["w_ff1"], w["w_ff2"])
        x2d = _ple(x1, f, p[i].reshape(tokens, PLE_DIM), w["w_ple_gate"], w["w_ple_proj"],
                   w["ln_mlp_post"], w["ln_ple_pre"], w["ln_ple_post"])
    return x2d.reshape(batch, seq_len, D_MODEL)


_MATRICES = ("w_in", "w_na_branch", "w_gqa_branch", "w_mix_out", "w_ff1", "w_ff2", "w_ple_gate",
             "w_ple_proj")
_VECTORS = ("ln_mix_pre", "q_norm", "k_norm", "ln_mix_post", "ln_mlp_pre", "ln_mlp_post",
            "ln_ple_pre", "ln_ple_post")


def kernel(x_prompt, x_sample, p_prompt, p_sample, ln_mix_pre, w_in, q_norm, k_norm, na_rpb, w_na_branch, w_gqa_branch, w_mix_out, ln_mix_post, ln_mlp_pre, w_ff1, w_ff2, ln_mlp_post, ln_ple_pre, w_ple_gate, w_ple_proj, ln_ple_post):
    named = dict(ln_mix_pre=ln_mix_pre, w_in=w_in, q_norm=q_norm, k_norm=k_norm,
                 w_na_branch=w_na_branch, w_gqa_branch=w_gqa_branch, w_mix_out=w_mix_out,
                 ln_mix_post=ln_mix_post, ln_mlp_pre=ln_mlp_pre, w_ff1=w_ff1, w_ff2=w_ff2,
                 ln_mlp_post=ln_mlp_post, ln_ple_pre=ln_ple_pre, w_ple_gate=w_ple_gate,
                 w_ple_proj=w_ple_proj, ln_ple_post=ln_ple_post)
    depth = w_in.shape[0]
    weights = []
    for i in range(depth):
        layer = {n: named[n][i].astype(BF16) for n in _MATRICES}
        layer.update({n: named[n][i].reshape(1, -1) for n in _VECTORS})
        weights.append(layer)
    bias_tables = [_na_bias_table(na_rpb[i]) for i in range(depth)]
    outs = []
    for x, p in ((x_prompt, p_prompt), (x_sample, p_sample)):
        rope = _rope_tables(x.shape[1])
        outs.append(_trunk(x, p, weights, bias_tables, rope))
    return tuple(outs)
```

```python
import functools

import jax
import jax.numpy as jnp
import numpy as np
from jax import lax
from jax.experimental import pallas as pl
from jax.experimental.pallas import tpu as pltpu

D_MODEL = 2048
GRID_W = 64
HEAD_DIM = 128
NA_HEADS = 8
GQA_HEADS = 8
GQA_KV_HEADS = 2
GQA_GROUP = GQA_HEADS // GQA_KV_HEADS
NA_WIDTH = NA_HEADS * HEAD_DIM
GQA_WIDTH = GQA_HEADS * HEAD_DIM
KV_WIDTH = GQA_KV_HEADS * HEAD_DIM
NA_ROWS_MAX = 8
NA_COLS = 16
D_FF = 4 * D_MODEL
PLE_DIM = 256
ROPE_THETA = 10000.0
EPS = 1e-6
NEG_INF = -1e30
ATTN_SCALE = HEAD_DIM ** -0.5
LOG2E = 1.4426950408889634
QUERY_SCALE = ATTN_SCALE * LOG2E

RPB_ROWS = 2 * NA_ROWS_MAX - 1
RPB_COLS = 2 * NA_COLS - 1

VMEM_LIMIT_BYTES = 56 * 1024 * 1024

BF16 = jnp.bfloat16
F32 = jnp.float32


def _params(*semantics):
    return pltpu.CompilerParams(dimension_semantics=semantics, vmem_limit_bytes=VMEM_LIMIT_BYTES)


def _resident(shape):
    zeros = (0,) * len(shape)
    return pl.BlockSpec(shape, lambda *_: zeros, pipeline_mode=pl.Buffered(1))


def _rms(x, gain):
    ms = jnp.mean(x * x, axis=-1, keepdims=True)
    return x * lax.rsqrt(ms + EPS) * gain


NA_QROWS = 4
NA_KROWS = NA_QROWS + NA_ROWS_MAX
NA_OFFSET_STEP = min(NA_QROWS, NA_ROWS_MAX // 2)
NA_BLOCK_TYPES = NA_ROWS_MAX // NA_OFFSET_STEP + 1
NA_BQ = NA_QROWS * GRID_W
NA_BK = NA_KROWS * GRID_W
assert NA_QROWS % NA_OFFSET_STEP == 0 and NA_KROWS % 2 == 0


def _na_block_rows(block_type, rr):
    offset = block_type * NA_OFFSET_STEP
    j0 = min(max(offset + rr - NA_ROWS_MAX // 2, 0), NA_KROWS - NA_ROWS_MAX)
    return j0, -offset - rr


def _na_bias_kernel(rpb_ref, o_ref):
    h = pl.program_id(0)
    qc = lax.broadcasted_iota(jnp.int32, (GRID_W, 2 * GRID_W), 0)
    lane = lax.broadcasted_iota(jnp.int32, (GRID_W, 2 * GRID_W), 1)
    upper = lane >= GRID_W
    kc = jnp.where(upper, lane - GRID_W, lane)
    dc = jnp.clip(kc - qc, -(NA_COLS - 1), NA_COLS - 1) + (NA_COLS - 1)
    col_start = jnp.clip(qc - NA_COLS // 2, 0, GRID_W - NA_COLS)
    col_ok = (kc >= col_start) & (kc < col_start + NA_COLS)
    base = h * (RPB_ROWS * RPB_COLS)
    neg = jnp.full((GRID_W, 2 * GRID_W), NEG_INF, F32)
    pairs = []
    for t in range(RPB_ROWS - 1):
        acc = jnp.zeros((GRID_W, 2 * GRID_W), F32)
        for d in range(RPB_COLS):
            lo = rpb_ref[base + t * RPB_COLS + d] * LOG2E
            hi = rpb_ref[base + (t + 1) * RPB_COLS + d] * LOG2E
            acc = jnp.where(dc == d, jnp.where(upper, hi, lo), acc)
        pairs.append(jnp.where(col_ok, acc, NEG_INF))
    for block_type in range(NA_BLOCK_TYPES):
        for rr in range(NA_QROWS):
            j0, dr0 = _na_block_rows(block_type, rr)
            for jp in range(NA_KROWS // 2):
                lo_ok = j0 <= 2 * jp < j0 + NA_ROWS_MAX
                hi_ok = j0 <= 2 * jp + 1 < j0 + NA_ROWS_MAX
                if not (lo_ok or hi_ok):
                    tile = neg
                else:
                    tile = pairs[dr0 + 2 * jp + (NA_ROWS_MAX - 1)]
                    if not lo_ok:
                        tile = jnp.where(upper, tile, NEG_INF)
                    if not hi_ok:
                        tile = jnp.where(upper, NEG_INF, tile)
                o_ref[0, block_type, rr * GRID_W:(rr + 1) * GRID_W,
                      jp * 2 * GRID_W:(jp + 1) * 2 * GRID_W] = tile


def _na_bias_table(rpb):
    shape = (NA_HEADS, NA_BLOCK_TYPES, NA_BQ, NA_BK)
    return pl.pallas_call(
        _na_bias_kernel,
        grid=(NA_HEADS,),
        in_specs=[pl.BlockSpec(memory_space=pltpu.SMEM)],
        out_specs=pl.BlockSpec((1,) + shape[1:], lambda h: (h, 0, 0, 0)),
        out_shape=jax.ShapeDtypeStruct(shape, F32),
        compiler_params=_params("arbitrary"),
        name="na_bias_table",
    )(rpb.reshape(-1))


INPROJ_TM = 256
INPROJ_TN = 512
_COL_GQ = 3 * NA_WIDTH
_COL_GK = _COL_GQ + GQA_WIDTH
_COL_GV = _COL_GK + KV_WIDTH
_COL_GATES = _COL_GV + KV_WIDTH
IN_COLS = _COL_GATES + 2 * D_MODEL


def _rope(y, cos, sin_lo, sin_hi):
    return y * cos + pltpu.roll(y, 32, 1) * sin_hi + pltpu.roll(y, HEAD_DIM - 32, 1) * sin_lo


def _inproj_kernel(x_ref, g_ref, w_ref, qn_ref, kn_ref, cos_ref, slo_ref, shi_ref,
                   na_ref, gq_ref, kv_ref, gt_ref, h_ref):
    h_ref[...] = _rms(x_ref[...], g_ref[...]).astype(BF16)

    def project(col, width=INPROJ_TN):
        return jnp.dot(h_ref[...], w_ref[:, col:col + width], preferred_element_type=F32)

    def norm_rope(acc, hh, gain_ref):
        sl = slice(hh * HEAD_DIM, (hh + 1) * HEAD_DIM)
        return _rope(_rms(acc[:, sl], gain_ref[...]), cos_ref[...], slo_ref[...], shi_ref[...])

    for col in range(0, 3 * NA_WIDTH, INPROJ_TN):
        acc = project(col)
        if col < NA_WIDTH:
            acc = acc * QUERY_SCALE
        for hh in range(INPROJ_TN // HEAD_DIM):
            na_ref[col // HEAD_DIM + hh] = acc[:, hh * HEAD_DIM:(hh + 1) * HEAD_DIM].astype(BF16)

    for col in range(0, GQA_WIDTH, INPROJ_TN):
        acc = project(_COL_GQ + col)
        for hh in range(INPROJ_TN // HEAD_DIM):
            out = slice(col + hh * HEAD_DIM, col + (hh + 1) * HEAD_DIM)
            gq_ref[:, out] = (norm_rope(acc, hh, qn_ref) * QUERY_SCALE).astype(BF16)

    acc = project(_COL_GK, 2 * KV_WIDTH)
    for hh in range(GQA_KV_HEADS):
        kv_ref[:, hh * HEAD_DIM:(hh + 1) * HEAD_DIM] = norm_rope(acc, hh, kn_ref).astype(BF16)
    kv_ref[:, KV_WIDTH:] = acc[:, KV_WIDTH:].astype(BF16)

    for col in range(0, 2 * D_MODEL, INPROJ_TN):
        gt_ref[:, col:col + INPROJ_TN] = jax.nn.sigmoid(project(_COL_GATES + col)).astype(BF16)


def _in_projection(x2d, seq_len, ln_pre, w_in, q_norm, k_norm, cos, sin_lo, sin_hi):
    tokens = x2d.shape[0]
    tm = INPROJ_TM
    assert w_in.shape == (D_MODEL, IN_COLS)
    seq_tiles = seq_len // tm
    row = lambda n: pl.BlockSpec((tm, n), lambda i: (i, 0))
    pos_spec = pl.BlockSpec((tm, HEAD_DIM), lambda i: (i % seq_tiles, 0))
    return pl.pallas_call(
        _inproj_kernel,
        grid=(tokens // tm,),
        in_specs=[
            row(D_MODEL), _resident((1, D_MODEL)), _resident((D_MODEL, IN_COLS)),
            _resident((1, HEAD_DIM)), _resident((1, HEAD_DIM)),
            pos_spec, pos_spec, pos_spec,
        ],
        out_specs=[pl.BlockSpec((3 * NA_HEADS, tm, HEAD_DIM), lambda i: (0, i, 0)),
                   row(GQA_WIDTH), row(2 * KV_WIDTH), row(2 * D_MODEL)],
        out_shape=[
            jax.ShapeDtypeStruct((3 * NA_HEADS, tokens, HEAD_DIM), BF16),
            jax.ShapeDtypeStruct((tokens, GQA_WIDTH), BF16),
            jax.ShapeDtypeStruct((tokens, 2 * KV_WIDTH), BF16),
            jax.ShapeDtypeStruct((tokens, 2 * D_MODEL), BF16),
        ],
        scratch_shapes=[pltpu.VMEM((tm, D_MODEL), BF16)],
        compiler_params=_params("parallel"),
        name="in_projection",
    )(x2d, ln_pre, w_in, q_norm, k_norm, cos, sin_lo, sin_hi)


def _na_kernel(q_ref, k_ref, v_ref, bias_ref, o_ref, *, rows):
    for blk in range(rows // NA_QROWS):
        key_row0 = min(max(NA_QROWS * blk - NA_ROWS_MAX // 2, 0), rows - NA_KROWS)
        block_type = (NA_QROWS * blk - key_row0) // NA_OFFSET_STEP
        queries = slice(blk * NA_BQ, (blk + 1) * NA_BQ)
        keys = slice(key_row0 * GRID_W, key_row0 * GRID_W + NA_BK)
        s = lax.dot_general(q_ref[0, 0, queries, :], k_ref[0, 0, keys, :],
                            (((1,), (1,)), ((), ())), preferred_element_type=F32)
        s = s + bias_ref[0, block_type]
        m = jnp.max(s, axis=-1, keepdims=True)
        p = jnp.exp2(s - m)
        l = jnp.sum(p, axis=-1, keepdims=True)
        o = jnp.dot(p.astype(BF16), v_ref[0, 0, keys, :], preferred_element_type=F32)
        o_ref[0, 0, queries, :] = (o / l).astype(BF16)


def _neighbourhood_attention(na_heads, bias):
    _, batch, seq_len, _ = na_heads.shape
    rows = seq_len // GRID_W
    assert rows % NA_QROWS == 0 and rows >= 2 * NA_ROWS_MAX
    head_spec = lambda off: pl.BlockSpec((1, 1, seq_len, HEAD_DIM), lambda h, b: (h + off, b, 0, 0))
    return pl.pallas_call(
        functools.partial(_na_kernel, rows=rows),
        grid=(NA_HEADS, batch),
        in_specs=[
            head_spec(0), head_spec(NA_HEADS), head_spec(2 * NA_HEADS),
            pl.BlockSpec((1, NA_BLOCK_TYPES, NA_BQ, NA_BK), lambda h, b: (h, 0, 0, 0)),
        ],
        out_specs=head_spec(0),
        out_shape=jax.ShapeDtypeStruct((NA_HEADS, batch, seq_len, HEAD_DIM), BF16),
        compiler_params=_params("parallel", "parallel"),
        name="neighbourhood_attention",
    )(na_heads, na_heads, na_heads, bias)


GQA_TQ = 1024


GQA_NK = 256


def _gqa_kernel(q_ref, k_ref, v_ref, o_ref, vext_ref):
    seq_len = k_ref.shape[1]

    @pl.when(pl.program_id(2) == 0)
    def _():
        vext_ref[:, :HEAD_DIM] = v_ref[0]
        vext_ref[:, HEAD_DIM:] = jnp.ones((seq_len, HEAD_DIM), BF16)

    tq = q_ref.shape[1]
    heads = [slice(g * HEAD_DIM, (g + 1) * HEAD_DIM) for g in range(GQA_GROUP)]
    q = jnp.concatenate([q_ref[0, :, sl] for sl in heads], axis=0)
    m = acc = None
    for c in range(seq_len // GQA_NK):
        keys = slice(c * GQA_NK, (c + 1) * GQA_NK)
        s = lax.dot_general(q, k_ref[0, keys, :], (((1,), (1,)), ((), ())),
                            preferred_element_type=F32)
        blocks = [s[:, n * HEAD_DIM:(n + 1) * HEAD_DIM] for n in range(GQA_NK // HEAD_DIM)]
        block_max = functools.reduce(jnp.maximum, blocks)
        row_max = jnp.broadcast_to(jnp.max(block_max, axis=-1, keepdims=True), block_max.shape)
        m_new = row_max if m is None else jnp.maximum(m, row_max)
        p = jnp.concatenate([jnp.exp2(b - m_new) for b in blocks], axis=1).astype(BF16)
        pv = jnp.dot(p, vext_ref[keys, :], preferred_element_type=F32)
        if acc is None:
            acc = pv
        else:
            alpha = jnp.exp2(m - m_new)
            acc = jnp.concatenate([acc[:, :HEAD_DIM] * alpha, acc[:, HEAD_DIM:] * alpha], axis=1) + pv
        m = m_new
    o = acc[:, :HEAD_DIM] / acc[:, HEAD_DIM:]
    for g, sl in enumerate(heads):
        o_ref[0, :, sl] = o[g * tq:(g + 1) * tq].astype(BF16)


def _gqa_attention(q3d, kv3d):
    batch, seq_len, _ = q3d.shape
    tq = GQA_TQ
    group_w = GQA_GROUP * HEAD_DIM
    return pl.pallas_call(
        _gqa_kernel,
        grid=(batch, GQA_KV_HEADS, seq_len // tq),
        in_specs=[
            pl.BlockSpec((1, tq, group_w), lambda b, kv, qi: (b, qi, kv)),
            pl.BlockSpec((1, seq_len, HEAD_DIM), lambda b, kv, qi: (b, 0, kv)),
            pl.BlockSpec((1, seq_len, HEAD_DIM), lambda b, kv, qi: (b, 0, GQA_KV_HEADS + kv)),
        ],
        out_specs=pl.BlockSpec((1, tq, group_w), lambda b, kv, qi: (b, qi, kv)),
        out_shape=jax.ShapeDtypeStruct((batch, seq_len, GQA_WIDTH), BF16),
        scratch_shapes=[pltpu.VMEM((seq_len, 2 * HEAD_DIM), BF16)],
        compiler_params=_params("parallel", "parallel", "arbitrary"),
        name="gqa_attention",
    )(q3d, kv3d, kv3d)


MERGE_TM = 512
MERGE_CHUNK = 256


def _merge_kernel(a_ref, b_ref, ga_ref, gb_ref, x_ref, wna_ref, wgq_ref, wmix_ref,
                  lnpost_ref, lnmlp_ref, x1_ref, h2_ref):
    for r in range(0, MERGE_TM, MERGE_CHUNK):
        rows = slice(r, r + MERGE_CHUNK)
        a = jnp.concatenate([a_ref[hh, rows, :] for hh in range(NA_HEADS)], axis=1)
        pa = jnp.dot(a, wna_ref[...], preferred_element_type=F32)
        pb = jnp.dot(b_ref[rows, :], wgq_ref[...], preferred_element_type=F32)
        merged = ga_ref[rows, :].astype(F32) * pa + gb_ref[rows, :].astype(F32) * pb
        y = jnp.dot(merged.astype(BF16), wmix_ref[...], preferred_element_type=F32)
        x1 = x_ref[rows, :] + _rms(y, lnpost_ref[...])
        x1_ref[rows, :] = x1
        h2_ref[rows, :] = _rms(x1, lnmlp_ref[...]).astype(BF16)


def _merge(a, b, gates, x2d, w_na, w_gq, w_mix, ln_post, ln_mlp):
    tokens = x2d.shape[0]
    tm = MERGE_TM
    row = lambda n, c=0: pl.BlockSpec((tm, n), lambda i: (i, c))
    return pl.pallas_call(
        _merge_kernel,
        grid=(tokens // tm,),
        in_specs=[
            pl.BlockSpec((NA_HEADS, tm, HEAD_DIM), lambda i: (0, i, 0)),
            row(GQA_WIDTH), row(D_MODEL, 0), row(D_MODEL, 1), row(D_MODEL),
            _resident((NA_WIDTH, D_MODEL)), _resident((GQA_WIDTH, D_MODEL)),
            _resident((D_MODEL, D_MODEL)), _resident((1, D_MODEL)), _resident((1, D_MODEL)),
        ],
        out_specs=[row(D_MODEL), row(D_MODEL)],
        out_shape=[jax.ShapeDtypeStruct((tokens, D_MODEL), F32),
                   jax.ShapeDtypeStruct((tokens, D_MODEL), BF16)],
        compiler_params=_params("parallel"),
        name="branch_merge",
    )(a, b, gates, gates, x2d, w_na, w_gq, w_mix, ln_post, ln_mlp)


MLP_TM = 1024
MLP_TK = 1024


MLP_CHUNK = 256


def _mlp_kernel(h_ref, w1_ref, w2_ref, f_ref):
    k = pl.program_id(1)

    def step(accumulate):
        for r in range(0, MLP_TM, MLP_CHUNK):
            rows = slice(r, r + MLP_CHUNK)
            u = jnp.maximum(jnp.dot(h_ref[rows, :], w1_ref[...], preferred_element_type=F32), 0.0)
            contrib = jnp.dot((u * u).astype(BF16), w2_ref[...], preferred_element_type=F32)
            if accumulate:
                f_ref[rows, :] += contrib
            else:
                f_ref[rows, :] = contrib

    @pl.when(k == 0)
    def _():
        step(False)

    @pl.when(k > 0)
    def _():
        step(True)


def _mlp(h2, w1, w2):
    tokens = h2.shape[0]
    tm, tk = MLP_TM, MLP_TK
    return pl.pallas_call(
        _mlp_kernel,
        grid=(tokens // tm, D_FF // tk),
        in_specs=[
            pl.BlockSpec((tm, D_MODEL), lambda i, k: (i, 0)),
            pl.BlockSpec((D_MODEL, tk), lambda i, k: (0, k)),
            pl.BlockSpec((tk, D_MODEL), lambda i, k: (k, 0)),
        ],
        out_specs=pl.BlockSpec((tm, D_MODEL), lambda i, k: (i, 0)),
        out_shape=jax.ShapeDtypeStruct((tokens, D_MODEL), F32),
        compiler_params=_params("parallel", "arbitrary"),
        name="mlp",
    )(h2, w1, w2)


PLE_TM = 512
PLE_CHUNK = 256
PLE_TN = 512


def _ple_kernel(x1_ref, f_ref, p_ref, wg_ref, wp_ref, lnpost_ref, lnpre_ref, lnple_ref, o_ref):
    for r in range(0, PLE_TM, PLE_CHUNK):
        rows = slice(r, r + PLE_CHUNK)
        x2 = x1_ref[rows, :] + _rms(f_ref[rows, :], lnpost_ref[...])
        hg = _rms(x2, lnpre_ref[...]).astype(BF16)
        pb = p_ref[rows, :].astype(BF16)
        tiles = []
        for c in range(0, D_MODEL, PLE_TN):
            cols = slice(c, c + PLE_TN)
            gate = jax.nn.sigmoid(jnp.dot(hg, wg_ref[:, cols], preferred_element_type=F32))
            tiles.append(jnp.dot(pb, wp_ref[:, cols], preferred_element_type=F32) * gate)
        e = jnp.concatenate(tiles, axis=1)
        o_ref[rows, :] = x2 + _rms(e, lnple_ref[...])


def _ple(x1, f, p2d, w_gate, w_proj, ln_mlp_post, ln_ple_pre, ln_ple_post):
    tokens = x1.shape[0]
    tm = PLE_TM
    row = lambda n: pl.BlockSpec((tm, n), lambda i: (i, 0))
    return pl.pallas_call(
        _ple_kernel,
        grid=(tokens // tm,),
        in_specs=[
            row(D_MODEL), row(D_MODEL), row(PLE_DIM),
            _resident((D_MODEL, D_MODEL)), _resident((PLE_DIM, D_MODEL)),
            _resident((1, D_MODEL)), _resident((1, D_MODEL)), _resident((1, D_MODEL)),
        ],
        out_specs=row(D_MODEL),
        out_shape=jax.ShapeDtypeStruct((tokens, D_MODEL), F32),
        compiler_params=_params("parallel"),
        name="ple",
    )(x1, f, p2d, w_gate, w_proj, ln_mlp_post, ln_ple_pre, ln_ple_post)


def _rope_tables(seq_len):
    half = HEAD_DIM // 2
    quarter = half // 2
    t = jnp.arange(seq_len)
    freqs = ROPE_THETA ** (-jnp.arange(0, half, 2, dtype=F32) / half)

    def axis_tables(pos):
        ang = pos.astype(F32)[:, None] * freqs[None, :]
        cos, sin = jnp.cos(ang), jnp.sin(ang)
        zero = jnp.zeros_like(sin)
        return (jnp.concatenate([cos, cos], -1), jnp.concatenate([-sin, zero], -1),
                jnp.concatenate([zero, sin], -1))

    row_t = axis_tables(t // GRID_W)
    col_t = axis_tables(t % GRID_W)
    assert quarter * 4 == HEAD_DIM
    return tuple(jnp.concatenate([a, b], -1) for a, b in zip(row_t, col_t))


def _trunk(x, p, weights, bias_tables, rope):
    batch, seq_len, _ = x.shape
    tokens = batch * seq_len
    x2d = x.reshape(tokens, D_MODEL)
    for i, w in enumerate(weights):
        na, gq, kv, gates = _in_projection(x2d, seq_len, w["ln_mix_pre"], w["w_in"], w["q_norm"],
                                           w["k_norm"], *rope)
        a = _neighbourhood_attention(na.reshape(3 * NA_HEADS, batch, seq_len, HEAD_DIM),
                                     bias_tables[i])
        b = _gqa_attention(gq.reshape(batch, seq_len, GQA_WIDTH),
                           kv.reshape(batch, seq_len, 2 * KV_WIDTH))
        x1, h2 = _merge(a.reshape(NA_HEADS, tokens, HEAD_DIM), b.reshape(tokens, GQA_WIDTH), gates, x2d,
                        w["w_na_branch"], w["w_gqa_branch"], w["w_mix_out"], w["ln_mix_post"],
                        w["ln_mlp_pre"])
        f = _mlp(h2, w["w_ff1"], w["w_ff2"])
        x2d = _ple(x1, f, p[i].reshape(tokens, PLE_DIM), w["w_ple_gate"], w["w_ple_proj"],
                   w["ln_mlp_post"], w["ln_ple_pre"], w["ln_ple_post"])
    return x2d.reshape(batch, seq_len, D_MODEL)


_MATRICES = ("w_in", "w_na_branch", "w_gqa_branch", "w_mix_out", "w_ff1", "w_ff2", "w_ple_gate",
             "w_ple_proj")
_VECTORS = ("ln_mix_pre", "q_norm", "k_norm", "ln_mix_post", "ln_mlp_pre", "ln_mlp_post",
            "ln_ple_pre", "ln_ple_post")


def kernel(x_prompt, x_sample, p_prompt, p_sample, ln_mix_pre, w_in, q_norm, k_norm, na_rpb, w_na_branch, w_gqa_branch, w_mix_out, ln_mix_post, ln_mlp_pre, w_ff1, w_ff2, ln_mlp_post, ln_ple_pre, w_ple_gate, w_ple_proj, ln_ple_post):
    named = dict(ln_mix_pre=ln_mix_pre, w_in=w_in, q_norm=q_norm, k_norm=k_norm,
                 w_na_branch=w_na_branch, w_gqa_branch=w_gqa_branch, w_mix_out=w_mix_out,
                 ln_mix_post=ln_mix_post, ln_mlp_pre=ln_mlp_pre, w_ff1=w_ff1, w_ff2=w_ff2,
                 ln_mlp_post=ln_mlp_post, ln_ple_pre=ln_ple_pre, w_ple_gate=w_ple_gate,
                 w_ple_proj=w_ple_proj, ln_ple_post=ln_ple_post)
    depth = w_in.shape[0]
    weights = []
    for i in range(depth):
        layer = {n: named[n][i].astype(BF16) for n in _MATRICES}
        layer.update({n: named[n][i].reshape(1, -1) for n in _VECTORS})
        weights.append(layer)
    bias_tables = [_na_bias_table(na_rpb[i]) for i in range(depth)]
    outs = []
    for x, p in ((x_prompt, p_prompt), (x_sample, p_sample)):
        rope = _rope_tables(x.shape[1])
        outs.append(_trunk(x, p, weights, bias_tables, rope))
    return tuple(outs)
```

```python
import functools

import jax
import jax.numpy as jnp
from jax import lax
from jax.experimental import pallas as pl
from jax.experimental.pallas import tpu as pltpu

D_MODEL = 2048
GRID_W = 64
HEAD_DIM = 128
NA_HEADS = 8
GQA_HEADS = 8
GQA_KV_HEADS = 2
GQA_GROUP = GQA_HEADS // GQA_KV_HEADS
NA_WIDTH = NA_HEADS * HEAD_DIM
GQA_WIDTH = GQA_HEADS * HEAD_DIM
KV_WIDTH = GQA_KV_HEADS * HEAD_DIM
NA_ROWS_MAX = 8
NA_COLS = 16
D_FF = 4 * D_MODEL
PLE_DIM = 256
ROPE_THETA = 10000.0
EPS = 1e-6
NEG_INF = -1e30
ATTN_SCALE = HEAD_DIM ** -0.5
LOG2E = 1.4426950408889634
QUERY_SCALE = ATTN_SCALE * LOG2E

RPB_ROWS = 2 * NA_ROWS_MAX - 1
RPB_COLS = 2 * NA_COLS - 1

VMEM_LIMIT_BYTES = 56 * 1024 * 1024

BF16 = jnp.bfloat16
F32 = jnp.float32


def _params(*semantics):
    return pltpu.CompilerParams(dimension_semantics=semantics, vmem_limit_bytes=VMEM_LIMIT_BYTES)


def _resident(shape):
    zeros = (0,) * len(shape)
    return pl.BlockSpec(shape, lambda *_: zeros, pipeline_mode=pl.Buffered(1))


def _rms(x, gain):
    ms = jnp.mean(x * x, axis=-1, keepdims=True)
    return x * lax.rsqrt(ms + EPS) * gain


NA_QROWS = 4
NA_KROWS = NA_QROWS + NA_ROWS_MAX
NA_OFFSET_STEP = min(NA_QROWS, NA_ROWS_MAX // 2)
NA_BLOCK_TYPES = NA_ROWS_MAX // NA_OFFSET_STEP + 1
NA_BQ = NA_QROWS * GRID_W
NA_BK = NA_KROWS * GRID_W
assert NA_QROWS % NA_OFFSET_STEP == 0 and NA_KROWS % 2 == 0


def _na_block_rows(block_type, rr):
    offset = block_type * NA_OFFSET_STEP
    j0 = min(max(offset + rr - NA_ROWS_MAX // 2, 0), NA_KROWS - NA_ROWS_MAX)
    return j0, -offset - rr


def _na_bias_kernel(rpb_ref, o_ref):
    h = pl.program_id(0)
    qc = lax.broadcasted_iota(jnp.int32, (GRID_W, 2 * GRID_W), 0)
    lane = lax.broadcasted_iota(jnp.int32, (GRID_W, 2 * GRID_W), 1)
    upper = lane >= GRID_W
    kc = jnp.where(upper, lane - GRID_W, lane)
    dc = jnp.clip(kc - qc, -(NA_COLS - 1), NA_COLS - 1) + (NA_COLS - 1)
    col_start = jnp.clip(qc - NA_COLS // 2, 0, GRID_W - NA_COLS)
    col_ok = (kc >= col_start) & (kc < col_start + NA_COLS)
    base = h * (RPB_ROWS * RPB_COLS)
    neg = jnp.full((GRID_W, 2 * GRID_W), NEG_INF, F32)
    pairs = []
    for t in range(RPB_ROWS - 1):
        acc = jnp.zeros((GRID_W, 2 * GRID_W), F32)
        for d in range(RPB_COLS):
            lo = rpb_ref[base + t * RPB_COLS + d] * LOG2E
            hi = rpb_ref[base + (t + 1) * RPB_COLS + d] * LOG2E
            acc = jnp.where(dc == d, jnp.where(upper, hi, lo), acc)
        pairs.append(jnp.where(col_ok, acc, NEG_INF))
    for block_type in range(NA_BLOCK_TYPES):
        for rr in range(NA_QROWS):
            j0, dr0 = _na_block_rows(block_type, rr)
            for jp in range(NA_KROWS // 2):
                lo_ok = j0 <= 2 * jp < j0 + NA_ROWS_MAX
                hi_ok = j0 <= 2 * jp + 1 < j0 + NA_ROWS_MAX
                if not (lo_ok or hi_ok):
                    tile = neg
                else:
                    tile = pairs[dr0 + 2 * jp + (NA_ROWS_MAX - 1)]
                    if not lo_ok:
                        tile = jnp.where(upper, tile, NEG_INF)
                    if not hi_ok:
                        tile = jnp.where(upper, NEG_INF, tile)
                o_ref[0, block_type, rr * GRID_W:(rr + 1) * GRID_W,
                      jp * 2 * GRID_W:(jp + 1) * 2 * GRID_W] = tile


def _na_bias_table(rpb):
    shape = (NA_HEADS, NA_BLOCK_TYPES, NA_BQ, NA_BK)
    return pl.pallas_call(
        _na_bias_kernel,
        grid=(NA_HEADS,),
        in_specs=[pl.BlockSpec(memory_space=pltpu.SMEM)],
        out_specs=pl.BlockSpec((1,) + shape[1:], lambda h: (h, 0, 0, 0)),
        out_shape=jax.ShapeDtypeStruct(shape, F32),
        compiler_params=_params("arbitrary"),
        name="na_bias_table",
    )(rpb.reshape(-1))


INPROJ_TM = 256
INPROJ_TN = 512
_COL_GQ = 3 * NA_WIDTH
_COL_GK = _COL_GQ + GQA_WIDTH
_COL_GV = _COL_GK + KV_WIDTH
_COL_GATES = _COL_GV + KV_WIDTH
IN_COLS = _COL_GATES + 2 * D_MODEL


ROPE_PAIR = HEAD_DIM // 4


def _rope(y, cos, sin_lo, sin_hi):
    return (y * cos + pltpu.roll(y, ROPE_PAIR, 1) * sin_hi
            + pltpu.roll(y, HEAD_DIM - ROPE_PAIR, 1) * sin_lo)


def _inproj_kernel(x_ref, g_ref, w_ref, qn_ref, kn_ref, cos_ref, slo_ref, shi_ref,
                   na_ref, gq_ref, kv_ref, gt_ref, h_ref):
    h_ref[...] = _rms(x_ref[...], g_ref[...]).astype(BF16)

    def project(col, width=INPROJ_TN):
        return jnp.dot(h_ref[...], w_ref[:, col:col + width], preferred_element_type=F32)

    def norm_rope(acc, hh, gain_ref):
        sl = slice(hh * HEAD_DIM, (hh + 1) * HEAD_DIM)
        return _rope(_rms(acc[:, sl], gain_ref[...]), cos_ref[...], slo_ref[...], shi_ref[...])

    for col in range(0, 3 * NA_WIDTH, INPROJ_TN):
        acc = project(col)
        if col < NA_WIDTH:
            acc = acc * QUERY_SCALE
        for hh in range(INPROJ_TN // HEAD_DIM):
            na_ref[col // HEAD_DIM + hh] = acc[:, hh * HEAD_DIM:(hh + 1) * HEAD_DIM].astype(BF16)

    for col in range(0, GQA_WIDTH, INPROJ_TN):
        acc = project(_COL_GQ + col)
        for hh in range(INPROJ_TN // HEAD_DIM):
            out = slice(col + hh * HEAD_DIM, col + (hh + 1) * HEAD_DIM)
            gq_ref[:, out] = (norm_rope(acc, hh, qn_ref) * QUERY_SCALE).astype(BF16)

    acc = project(_COL_GK, 2 * KV_WIDTH)
    for hh in range(GQA_KV_HEADS):
        kv_ref[:, hh * HEAD_DIM:(hh + 1) * HEAD_DIM] = norm_rope(acc, hh, kn_ref).astype(BF16)
    kv_ref[:, KV_WIDTH:] = acc[:, KV_WIDTH:].astype(BF16)

    for col in range(0, 2 * D_MODEL, INPROJ_TN):
        gt_ref[:, col:col + INPROJ_TN] = jax.nn.sigmoid(project(_COL_GATES + col)).astype(BF16)


def _in_projection(x2d, seq_len, ln_pre, w_in, q_norm, k_norm, cos, sin_lo, sin_hi):
    tokens = x2d.shape[0]
    tm = INPROJ_TM
    assert w_in.shape == (D_MODEL, IN_COLS)
    seq_tiles = seq_len // tm
    row = lambda n: pl.BlockSpec((tm, n), lambda i: (i, 0))
    pos_spec = pl.BlockSpec((tm, HEAD_DIM), lambda i: (i % seq_tiles, 0))
    return pl.pallas_call(
        _inproj_kernel,
        grid=(tokens // tm,),
        in_specs=[
            row(D_MODEL), _resident((1, D_MODEL)), _resident((D_MODEL, IN_COLS)),
            _resident((1, HEAD_DIM)), _resident((1, HEAD_DIM)),
            pos_spec, pos_spec, pos_spec,
        ],
        out_specs=[pl.BlockSpec((3 * NA_HEADS, tm, HEAD_DIM), lambda i: (0, i, 0)),
                   row(GQA_WIDTH), row(2 * KV_WIDTH), row(2 * D_MODEL)],
        out_shape=[
            jax.ShapeDtypeStruct((3 * NA_HEADS, tokens, HEAD_DIM), BF16),
            jax.ShapeDtypeStruct((tokens, GQA_WIDTH), BF16),
            jax.ShapeDtypeStruct((tokens, 2 * KV_WIDTH), BF16),
            jax.ShapeDtypeStruct((tokens, 2 * D_MODEL), BF16),
        ],
        scratch_shapes=[pltpu.VMEM((tm, D_MODEL), BF16)],
        compiler_params=_params("parallel"),
        name="in_projection",
    )(x2d, ln_pre, w_in, q_norm, k_norm, cos, sin_lo, sin_hi)


def _na_kernel(q_ref, k_ref, v_ref, bias_ref, o_ref, *, rows):
    for blk in range(rows // NA_QROWS):
        key_row0 = min(max(NA_QROWS * blk - NA_ROWS_MAX // 2, 0), rows - NA_KROWS)
        block_type = (NA_QROWS * blk - key_row0) // NA_OFFSET_STEP
        queries = slice(blk * NA_BQ, (blk + 1) * NA_BQ)
        keys = slice(key_row0 * GRID_W, key_row0 * GRID_W + NA_BK)
        s = lax.dot_general(q_ref[0, 0, queries, :], k_ref[0, 0, keys, :],
                            (((1,), (1,)), ((), ())), preferred_element_type=F32)
        s = s + bias_ref[0, block_type]
        m = jnp.max(s, axis=-1, keepdims=True)
        p = jnp.exp2(s - m)
        l = jnp.sum(p, axis=-1, keepdims=True)
        o = jnp.dot(p.astype(BF16), v_ref[0, 0, keys, :], preferred_element_type=F32)
        o_ref[0, 0, queries, :] = (o / l).astype(BF16)


def _neighbourhood_attention(na_heads, bias):
    _, batch, seq_len, _ = na_heads.shape
    rows = seq_len // GRID_W
    assert rows % NA_QROWS == 0 and rows >= 2 * NA_ROWS_MAX
    head_spec = lambda off: pl.BlockSpec((1, 1, seq_len, HEAD_DIM), lambda h, b: (h + off, b, 0, 0))
    return pl.pallas_call(
        functools.partial(_na_kernel, rows=rows),
        grid=(NA_HEADS, batch),
        in_specs=[
            head_spec(0), head_spec(NA_HEADS), head_spec(2 * NA_HEADS),
            pl.BlockSpec((1, NA_BLOCK_TYPES, NA_BQ, NA_BK), lambda h, b: (h, 0, 0, 0)),
        ],
        out_specs=head_spec(0),
        out_shape=jax.ShapeDtypeStruct((NA_HEADS, batch, seq_len, HEAD_DIM), BF16),
        compiler_params=_params("parallel", "parallel"),
        name="neighbourhood_attention",
    )(na_heads, na_heads, na_heads, bias)


GQA_TQ = 1024
GQA_NK = 256


def _gqa_kernel(q_ref, k_ref, v_ref, o_ref, vext_ref):
    seq_len = k_ref.shape[1]

    @pl.when(pl.program_id(2) == 0)
    def _():
        vext_ref[:, :HEAD_DIM] = v_ref[0]
        vext_ref[:, HEAD_DIM:] = jnp.ones((seq_len, HEAD_DIM), BF16)

    tq = q_ref.shape[1]
    heads = [slice(g * HEAD_DIM, (g + 1) * HEAD_DIM) for g in range(GQA_GROUP)]
    q = jnp.concatenate([q_ref[0, :, sl] for sl in heads], axis=0)
    m = acc = None
    for c in range(seq_len // GQA_NK):
        keys = slice(c * GQA_NK, (c + 1) * GQA_NK)
        s = lax.dot_general(q, k_ref[0, keys, :], (((1,), (1,)), ((), ())),
                            preferred_element_type=F32)
        blocks = [s[:, n * HEAD_DIM:(n + 1) * HEAD_DIM] for n in range(GQA_NK // HEAD_DIM)]
        block_max = functools.reduce(jnp.maximum, blocks)
        row_max = jnp.broadcast_to(jnp.max(block_max, axis=-1, keepdims=True), block_max.shape)
        m_new = row_max if m is None else jnp.maximum(m, row_max)
        p = jnp.concatenate([jnp.exp2(b - m_new) for b in blocks], axis=1).astype(BF16)
        pv = jnp.dot(p, vext_ref[keys, :], preferred_element_type=F32)
        if acc is None:
            acc = pv
        else:
            alpha = jnp.exp2(m - m_new)
            acc = jnp.concatenate([acc[:, :HEAD_DIM] * alpha, acc[:, HEAD_DIM:] * alpha], axis=1) + pv
        m = m_new
    o = acc[:, :HEAD_DIM] / acc[:, HEAD_DIM:]
    for g, sl in enumerate(heads):
        o_ref[0, :, sl] = o[g * tq:(g + 1) * tq].astype(BF16)


def _gqa_attention(q3d, kv3d):
    batch, seq_len, _ = q3d.shape
    tq = GQA_TQ
    group_w = GQA_GROUP * HEAD_DIM
    return pl.pallas_call(
        _gqa_kernel,
        grid=(batch, GQA_KV_HEADS, seq_len // tq),
        in_specs=[
            pl.BlockSpec((1, tq, group_w), lambda b, kv, qi: (b, qi, kv)),
            pl.BlockSpec((1, seq_len, HEAD_DIM), lambda b, kv, qi: (b, 0, kv)),
            pl.BlockSpec((1, seq_len, HEAD_DIM), lambda b, kv, qi: (b, 0, GQA_KV_HEADS + kv)),
        ],
        out_specs=pl.BlockSpec((1, tq, group_w), lambda b, kv, qi: (b, qi, kv)),
        out_shape=jax.ShapeDtypeStruct((batch, seq_len, GQA_WIDTH), BF16),
        scratch_shapes=[pltpu.VMEM((seq_len, 2 * HEAD_DIM), BF16)],
        compiler_params=_params("parallel", "parallel", "arbitrary"),
        name="gqa_attention",
    )(q3d, kv3d, kv3d)


MERGE_TM = 512
MERGE_CHUNK = 256


def _merge_kernel(a_ref, b_ref, ga_ref, gb_ref, x_ref, wna_ref, wgq_ref, wmix_ref,
                  lnpost_ref, lnmlp_ref, x1_ref, h2_ref):
    for r in range(0, MERGE_TM, MERGE_CHUNK):
        rows = slice(r, r + MERGE_CHUNK)
        a = jnp.concatenate([a_ref[hh, rows, :] for hh in range(NA_HEADS)], axis=1)
        pa = jnp.dot(a, wna_ref[...], preferred_element_type=F32)
        pb = jnp.dot(b_ref[rows, :], wgq_ref[...], preferred_element_type=F32)
        merged = ga_ref[rows, :].astype(F32) * pa + gb_ref[rows, :].astype(F32) * pb
        y = jnp.dot(merged.astype(BF16), wmix_ref[...], preferred_element_type=F32)
        x1 = x_ref[rows, :] + _rms(y, lnpost_ref[...])
        x1_ref[rows, :] = x1
        h2_ref[rows, :] = _rms(x1, lnmlp_ref[...]).astype(BF16)


def _merge(a, b, gates, x2d, w_na, w_gq, w_mix, ln_post, ln_mlp):
    tokens = x2d.shape[0]
    tm = MERGE_TM
    row = lambda n, c=0: pl.BlockSpec((tm, n), lambda i: (i, c))
    return pl.pallas_call(
        _merge_kernel,
        grid=(tokens // tm,),
        in_specs=[
            pl.BlockSpec((NA_HEADS, tm, HEAD_DIM), lambda i: (0, i, 0)),
            row(GQA_WIDTH), row(D_MODEL, 0), row(D_MODEL, 1), row(D_MODEL),
            _resident((NA_WIDTH, D_MODEL)), _resident((GQA_WIDTH, D_MODEL)),
            _resident((D_MODEL, D_MODEL)), _resident((1, D_MODEL)), _resident((1, D_MODEL)),
        ],
        out_specs=[row(D_MODEL), row(D_MODEL)],
        out_shape=[jax.ShapeDtypeStruct((tokens, D_MODEL), F32),
                   jax.ShapeDtypeStruct((tokens, D_MODEL), BF16)],
        compiler_params=_params("parallel"),
        name="branch_merge",
    )(a, b, gates, gates, x2d, w_na, w_gq, w_mix, ln_post, ln_mlp)


MLP_TM = 1024
MLP_TK = 1024
MLP_CHUNK = 256


def _mlp_kernel(h_ref, w1_ref, w2_ref, f_ref):
    k = pl.program_id(1)

    def step(accumulate):
        for r in range(0, MLP_TM, MLP_CHUNK):
            rows = slice(r, r + MLP_CHUNK)
            u = jnp.maximum(jnp.dot(h_ref[rows, :], w1_ref[...], preferred_element_type=F32), 0.0)
            contrib = jnp.dot((u * u).astype(BF16), w2_ref[...], preferred_element_type=F32)
            if accumulate:
                f_ref[rows, :] += contrib
            else:
                f_ref[rows, :] = contrib

    @pl.when(k == 0)
    def _():
        step(False)

    @pl.when(k > 0)
    def _():
        step(True)


def _mlp(h2, w1, w2):
    tokens = h2.shape[0]
    tm, tk = MLP_TM, MLP_TK
    return pl.pallas_call(
        _mlp_kernel,
        grid=(tokens // tm, D_FF // tk),
        in_specs=[
            pl.BlockSpec((tm, D_MODEL), lambda i, k: (i, 0)),
            pl.BlockSpec((D_MODEL, tk), lambda i, k: (0, k)),
            pl.BlockSpec((tk, D_MODEL), lambda i, k: (k, 0)),
        ],
        out_specs=pl.BlockSpec((tm, D_MODEL), lambda i, k: (i, 0)),
        out_shape=jax.ShapeDtypeStruct((tokens, D_MODEL), F32),
        compiler_params=_params("parallel", "arbitrary"),
        name="mlp",
    )(h2, w1, w2)


PLE_TM = 512
PLE_CHUNK = 256
PLE_TN = 512


def _ple_kernel(x1_ref, f_ref, p_ref, wg_ref, wp_ref, lnpost_ref, lnpre_ref, lnple_ref, o_ref):
    for r in range(0, PLE_TM, PLE_CHUNK):
        rows = slice(r, r + PLE_CHUNK)
        x2 = x1_ref[rows, :] + _rms(f_ref[rows, :], lnpost_ref[...])
        hg = _rms(x2, lnpre_ref[...]).astype(BF16)
        pb = p_ref[rows, :].astype(BF16)
        tiles = []
        for c in range(0, D_MODEL, PLE_TN):
            cols = slice(c, c + PLE_TN)
            gate = jax.nn.sigmoid(jnp.dot(hg, wg_ref[:, cols], preferred_element_type=F32))
            tiles.append(jnp.dot(pb, wp_ref[:, cols], preferred_element_type=F32) * gate)
        e = jnp.concatenate(tiles, axis=1)
        o_ref[rows, :] = x2 + _rms(e, lnple_ref[...])


def _ple(x1, f, p2d, w_gate, w_proj, ln_mlp_post, ln_ple_pre, ln_ple_post):
    tokens = x1.shape[0]
    tm = PLE_TM
    row = lambda n: pl.BlockSpec((tm, n), lambda i: (i, 0))
    return pl.pallas_call(
        _ple_kernel,
        grid=(tokens // tm,),
        in_specs=[
            row(D_MODEL), row(D_MODEL), row(PLE_DIM),
            _resident((D_MODEL, D_MODEL)), _resident((PLE_DIM, D_MODEL)),
            _resident((1, D_MODEL)), _resident((1, D_MODEL)), _resident((1, D_MODEL)),
        ],
        out_specs=row(D_MODEL),
        out_shape=jax.ShapeDtypeStruct((tokens, D_MODEL), F32),
        compiler_params=_params("parallel"),
        name="ple",
    )(x1, f, p2d, w_gate, w_proj, ln_mlp_post, ln_ple_pre, ln_ple_post)


def _rope_tables(seq_len):
    half = HEAD_DIM // 2
    quarter = half // 2
    t = jnp.arange(seq_len)
    freqs = ROPE_THETA ** (-jnp.arange(0, half, 2, dtype=F32) / half)

    def axis_tables(pos):
        ang = pos.astype(F32)[:, None] * freqs[None, :]
        cos, sin = jnp.cos(ang), jnp.sin(ang)
        zero = jnp.zeros_like(sin)
        return (jnp.concatenate([cos, cos], -1), jnp.concatenate([-sin, zero], -1),
                jnp.concatenate([zero, sin], -1))

    row_t = axis_tables(t // GRID_W)
    col_t = axis_tables(t % GRID_W)
    assert quarter * 4 == HEAD_DIM
    return tuple(jnp.concatenate([a, b], -1) for a, b in zip(row_t, col_t))


def _trunk(x, p, weights, bias_tables, rope):
    batch, seq_len, _ = x.shape
    tokens = batch * seq_len
    x2d = x.reshape(tokens, D_MODEL)
    for i, w in enumerate(weights):
        na, gq, kv, gates = _in_projection(x2d, seq_len, w["ln_mix_pre"], w["w_in"], w["q_norm"],
                                           w["k_norm"], *rope)
        a = _neighbourhood_attention(na.reshape(3 * NA_HEADS, batch, seq_len, HEAD_DIM),
                                     bias_tables[i])
        b = _gqa_attention(gq.reshape(batch, seq_len, GQA_WIDTH),
                           kv.reshape(batch, seq_len, 2 * KV_WIDTH))
        x1, h2 = _merge(a.reshape(NA_HEADS, tokens, HEAD_DIM), b.reshape(tokens, GQA_WIDTH), gates, x2d,
                        w["w_na_branch"], w["w_gqa_branch"], w["w_mix_out"], w["ln_mix_post"],
                        w["ln_mlp_pre"])
        f = _mlp(h2, w["w_ff1"], w["w_ff2"])
        x2d = _ple(x1, f, p[i].reshape(tokens, PLE_DIM), w["w_ple_gate"], w["w_ple_proj"],
                   w["ln_mlp_post"], w["ln_ple_pre"], w["ln_ple_post"])
    return x2d.reshape(batch, seq_len, D_MODEL)


_MATRICES = ("w_in", "w_na_branch", "w_gqa_branch", "w_mix_out", "w_ff1", "w_ff2", "w_ple_gate",
             "w_ple_proj")
_VECTORS = ("ln_mix_pre", "q_norm", "k_norm", "ln_mix_post", "ln_mlp_pre", "ln_mlp_post",
            "ln_ple_pre", "ln_ple_post")


def kernel(x_prompt, x_sample, p_prompt, p_sample, ln_mix_pre, w_in, q_norm, k_norm, na_rpb, w_na_branch, w_gqa_branch, w_mix_out, ln_mix_post, ln_mlp_pre, w_ff1, w_ff2, ln_mlp_post, ln_ple_pre, w_ple_gate, w_ple_proj, ln_ple_post):
    named = dict(ln_mix_pre=ln_mix_pre, w_in=w_in, q_norm=q_norm, k_norm=k_norm,
                 w_na_branch=w_na_branch, w_gqa_branch=w_gqa_branch, w_mix_out=w_mix_out,
                 ln_mix_post=ln_mix_post, ln_mlp_pre=ln_mlp_pre, w_ff1=w_ff1, w_ff2=w_ff2,
                 ln_mlp_post=ln_mlp_post, ln_ple_pre=ln_ple_pre, w_ple_gate=w_ple_gate,
                 w_ple_proj=w_ple_proj, ln_ple_post=ln_ple_post)
    depth = w_in.shape[0]
    weights = []
    for i in range(depth):
        layer = {n: named[n][i].astype(BF16) for n in _MATRICES}
        layer.update({n: named[n][i].reshape(1, -1) for n in _VECTORS})
        weights.append(layer)
    bias_tables = [_na_bias_table(na_rpb[i]) for i in range(depth)]
    outs = []
    for x, p in ((x_prompt, p_prompt), (x_sample, p_sample)):
        rope = _rope_tables(x.shape[1])
        outs.append(_trunk(x, p, weights, bias_tables, rope))
    return tuple(outs)
```

```python
import functools

import jax
import jax.numpy as jnp
from jax import lax
from jax.experimental import pallas as pl
from jax.experimental.pallas import tpu as pltpu

D_MODEL = 2048
GRID_W = 64
HEAD_DIM = 128
NA_HEADS = 8
GQA_HEADS = 8
GQA_KV_HEADS = 2
GQA_GROUP = GQA_HEADS // GQA_KV_HEADS
NA_WIDTH = NA_HEADS * HEAD_DIM
GQA_WIDTH = GQA_HEADS * HEAD_DIM
KV_WIDTH = GQA_KV_HEADS * HEAD_DIM
NA_ROWS_MAX = 8
NA_COLS = 16
D_FF = 4 * D_MODEL
PLE_DIM = 256
ROPE_THETA = 10000.0
EPS = 1e-6
NEG_INF = -1e30
ATTN_SCALE = HEAD_DIM ** -0.5
LOG2E = 1.4426950408889634
QUERY_SCALE = ATTN_SCALE * LOG2E

RPB_ROWS = 2 * NA_ROWS_MAX - 1
RPB_COLS = 2 * NA_COLS - 1

VMEM_LIMIT_BYTES = 56 * 1024 * 1024

BF16 = jnp.bfloat16
F32 = jnp.float32


def _params(*semantics):
    return pltpu.CompilerParams(dimension_semantics=semantics, vmem_limit_bytes=VMEM_LIMIT_BYTES)


def _resident(shape):
    zeros = (0,) * len(shape)
    return pl.BlockSpec(shape, lambda *_: zeros, pipeline_mode=pl.Buffered(1))


def _rms(x, gain):
    ms = jnp.mean(x * x, axis=-1, keepdims=True)
    return x * lax.rsqrt(ms + EPS) * gain


NA_QROWS = 4
NA_KROWS = NA_QROWS + NA_ROWS_MAX
NA_OFFSET_STEP = min(NA_QROWS, NA_ROWS_MAX // 2)
NA_BLOCK_TYPES = NA_ROWS_MAX // NA_OFFSET_STEP + 1
NA_BQ = NA_QROWS * GRID_W
NA_BK = NA_KROWS * GRID_W
assert NA_QROWS % NA_OFFSET_STEP == 0 and NA_KROWS % 2 == 0


def _na_block_rows(block_type, rr):
    offset = block_type * NA_OFFSET_STEP
    j0 = min(max(offset + rr - NA_ROWS_MAX // 2, 0), NA_KROWS - NA_ROWS_MAX)
    return j0, -offset - rr


def _na_bias_kernel(rpb_ref, o_ref):
    h = pl.program_id(0)
    qc = lax.broadcasted_iota(jnp.int32, (GRID_W, 2 * GRID_W), 0)
    lane = lax.broadcasted_iota(jnp.int32, (GRID_W, 2 * GRID_W), 1)
    upper = lane >= GRID_W
    kc = jnp.where(upper, lane - GRID_W, lane)
    dc = jnp.clip(kc - qc, -(NA_COLS - 1), NA_COLS - 1) + (NA_COLS - 1)
    col_start = jnp.clip(qc - NA_COLS // 2, 0, GRID_W - NA_COLS)
    col_ok = (kc >= col_start) & (kc < col_start + NA_COLS)
    base = h * (RPB_ROWS * RPB_COLS)
    neg = jnp.full((GRID_W, 2 * GRID_W), NEG_INF, F32)
    pairs = []
    for t in range(RPB_ROWS - 1):
        acc = jnp.zeros((GRID_W, 2 * GRID_W), F32)
        for d in range(RPB_COLS):
            lo = rpb_ref[base + t * RPB_COLS + d] * LOG2E
            hi = rpb_ref[base + (t + 1) * RPB_COLS + d] * LOG2E
            acc = jnp.where(dc == d, jnp.where(upper, hi, lo), acc)
        pairs.append(jnp.where(col_ok, acc, NEG_INF))
    for block_type in range(NA_BLOCK_TYPES):
        for rr in range(NA_QROWS):
            j0, dr0 = _na_block_rows(block_type, rr)
            for jp in range(NA_KROWS // 2):
                lo_ok = j0 <= 2 * jp < j0 + NA_ROWS_MAX
                hi_ok = j0 <= 2 * jp + 1 < j0 + NA_ROWS_MAX
                if not (lo_ok or hi_ok):
                    tile = neg
                else:
                    tile = pairs[dr0 + 2 * jp + (NA_ROWS_MAX - 1)]
                    if not lo_ok:
                        tile = jnp.where(upper, tile, NEG_INF)
                    if not hi_ok:
                        tile = jnp.where(upper, NEG_INF, tile)
                o_ref[0, block_type, rr * GRID_W:(rr + 1) * GRID_W,
                      jp * 2 * GRID_W:(jp + 1) * 2 * GRID_W] = tile


def _na_bias_table(rpb):
    shape = (NA_HEADS, NA_BLOCK_TYPES, NA_BQ, NA_BK)
    return pl.pallas_call(
        _na_bias_kernel,
        grid=(NA_HEADS,),
        in_specs=[pl.BlockSpec(memory_space=pltpu.SMEM)],
        out_specs=pl.BlockSpec((1,) + shape[1:], lambda h: (h, 0, 0, 0)),
        out_shape=jax.ShapeDtypeStruct(shape, F32),
        compiler_params=_params("arbitrary"),
        name="na_bias_table",
    )(rpb.reshape(-1))


INPROJ_TM = 256
INPROJ_TN = 512
_COL_GQ = 3 * NA_WIDTH
_COL_GK = _COL_GQ + GQA_WIDTH
_COL_GV = _COL_GK + KV_WIDTH
_COL_GATES = _COL_GV + KV_WIDTH
IN_COLS = _COL_GATES + 2 * D_MODEL


ROPE_PAIR = HEAD_DIM // 4


def _rope(y, cos, sin_lo, sin_hi):
    return (y * cos + pltpu.roll(y, ROPE_PAIR, 1) * sin_hi
            + pltpu.roll(y, HEAD_DIM - ROPE_PAIR, 1) * sin_lo)


def _inproj_kernel(x_ref, g_ref, w_ref, qn_ref, kn_ref, cos_ref, slo_ref, shi_ref,
                   na_ref, gq_ref, kv_ref, gt_ref, h_ref):
    h_ref[...] = _rms(x_ref[...], g_ref[...]).astype(BF16)

    def project(col, width=INPROJ_TN):
        return jnp.dot(h_ref[...], w_ref[:, col:col + width], preferred_element_type=F32)

    def norm_rope(acc, hh, gain_ref):
        sl = slice(hh * HEAD_DIM, (hh + 1) * HEAD_DIM)
        return _rope(_rms(acc[:, sl], gain_ref[...]), cos_ref[...], slo_ref[...], shi_ref[...])

    for col in range(0, 3 * NA_WIDTH, INPROJ_TN):
        acc = project(col)
        if col < NA_WIDTH:
            acc = acc * QUERY_SCALE
        for hh in range(INPROJ_TN // HEAD_DIM):
            na_ref[col // HEAD_DIM + hh] = acc[:, hh * HEAD_DIM:(hh + 1) * HEAD_DIM].astype(BF16)

    for col in range(0, GQA_WIDTH, INPROJ_TN):
        acc = project(_COL_GQ + col)
        for hh in range(INPROJ_TN // HEAD_DIM):
            out = slice(col + hh * HEAD_DIM, col + (hh + 1) * HEAD_DIM)
            gq_ref[:, out] = (norm_rope(acc, hh, qn_ref) * QUERY_SCALE).astype(BF16)

    acc = project(_COL_GK, 2 * KV_WIDTH)
    for hh in range(GQA_KV_HEADS):
        kv_ref[:, hh * HEAD_DIM:(hh + 1) * HEAD_DIM] = norm_rope(acc, hh, kn_ref).astype(BF16)
    kv_ref[:, KV_WIDTH:] = acc[:, KV_WIDTH:].astype(BF16)

    for col in range(0, 2 * D_MODEL, INPROJ_TN):
        gt_ref[:, col:col + INPROJ_TN] = jax.nn.sigmoid(project(_COL_GATES + col)).astype(BF16)


def _in_projection(x2d, seq_len, ln_pre, w_in, q_norm, k_norm, cos, sin_lo, sin_hi):
    tokens = x2d.shape[0]
    tm = INPROJ_TM
    assert w_in.shape == (D_MODEL, IN_COLS)
    seq_tiles = seq_len // tm
    row = lambda n: pl.BlockSpec((tm, n), lambda i: (i, 0))
    pos_spec = pl.BlockSpec((tm, HEAD_DIM), lambda i: (i % seq_tiles, 0))
    return pl.pallas_call(
        _inproj_kernel,
        grid=(tokens // tm,),
        in_specs=[
            row(D_MODEL), _resident((1, D_MODEL)), _resident((D_MODEL, IN_COLS)),
            _resident((1, HEAD_DIM)), _resident((1, HEAD_DIM)),
            pos_spec, pos_spec, pos_spec,
        ],
        out_specs=[pl.BlockSpec((3 * NA_HEADS, tm, HEAD_DIM), lambda i: (0, i, 0)),
                   row(GQA_WIDTH), row(2 * KV_WIDTH), row(2 * D_MODEL)],
        out_shape=[
            jax.ShapeDtypeStruct((3 * NA_HEADS, tokens, HEAD_DIM), BF16),
            jax.ShapeDtypeStruct((tokens, GQA_WIDTH), BF16),
            jax.ShapeDtypeStruct((tokens, 2 * KV_WIDTH), BF16),
            jax.ShapeDtypeStruct((tokens, 2 * D_MODEL), BF16),
        ],
        scratch_shapes=[pltpu.VMEM((tm, D_MODEL), BF16)],
        compiler_params=_params("parallel"),
        name="in_projection",
    )(x2d, ln_pre, w_in, q_norm, k_norm, cos, sin_lo, sin_hi)


def _na_kernel(q_ref, k_ref, v_ref, bias_ref, o_ref, vext_ref, *, rows):
    vext_ref[:, :HEAD_DIM] = v_ref[0, 0]
    vext_ref[:, HEAD_DIM:] = jnp.ones((rows * GRID_W, HEAD_DIM), BF16)
    for blk in range(rows // NA_QROWS):
        key_row0 = min(max(NA_QROWS * blk - NA_ROWS_MAX // 2, 0), rows - NA_KROWS)
        block_type = (NA_QROWS * blk - key_row0) // NA_OFFSET_STEP
        queries = slice(blk * NA_BQ, (blk + 1) * NA_BQ)
        keys = slice(key_row0 * GRID_W, key_row0 * GRID_W + NA_BK)
        s = lax.dot_general(q_ref[0, 0, queries, :], k_ref[0, 0, keys, :],
                            (((1,), (1,)), ((), ())), preferred_element_type=F32)
        s = s + bias_ref[0, block_type]
        m = jnp.max(s, axis=-1, keepdims=True)
        p = jnp.exp2(s - m)
        o = jnp.dot(p.astype(BF16), vext_ref[keys, :], preferred_element_type=F32)
        o_ref[0, 0, queries, :] = (o[:, :HEAD_DIM] / o[:, HEAD_DIM:]).astype(BF16)


def _neighbourhood_attention(na_heads, bias):
    _, batch, seq_len, _ = na_heads.shape
    rows = seq_len // GRID_W
    assert rows % NA_QROWS == 0 and rows >= 2 * NA_ROWS_MAX
    head_spec = lambda off: pl.BlockSpec((1, 1, seq_len, HEAD_DIM), lambda h, b: (h + off, b, 0, 0))
    return pl.pallas_call(
        functools.partial(_na_kernel, rows=rows),
        grid=(NA_HEADS, batch),
        in_specs=[
            head_spec(0), head_spec(NA_HEADS), head_spec(2 * NA_HEADS),
            pl.BlockSpec((1, NA_BLOCK_TYPES, NA_BQ, NA_BK), lambda h, b: (h, 0, 0, 0)),
        ],
        out_specs=head_spec(0),
        out_shape=jax.ShapeDtypeStruct((NA_HEADS, batch, seq_len, HEAD_DIM), BF16),
        scratch_shapes=[pltpu.VMEM((seq_len, 2 * HEAD_DIM), BF16)],
        compiler_params=_params("parallel", "parallel"),
        name="neighbourhood_attention",
    )(na_heads, na_heads, na_heads, bias)


GQA_TQ = 1024
GQA_NK = 256


def _gqa_kernel(q_ref, k_ref, v_ref, o_ref, vext_ref):
    seq_len = k_ref.shape[1]

    @pl.when(pl.program_id(2) == 0)
    def _():
        vext_ref[:, :HEAD_DIM] = v_ref[0]
        vext_ref[:, HEAD_DIM:] = jnp.ones((seq_len, HEAD_DIM), BF16)

    tq = q_ref.shape[1]
    heads = [slice(g * HEAD_DIM, (g + 1) * HEAD_DIM) for g in range(GQA_GROUP)]
    q = jnp.concatenate([q_ref[0, :, sl] for sl in heads], axis=0)
    m = acc = None
    for c in range(seq_len // GQA_NK):
        keys = slice(c * GQA_NK, (c + 1) * GQA_NK)
        s = lax.dot_general(q, k_ref[0, keys, :], (((1,), (1,)), ((), ())),
                            preferred_element_type=F32)
        blocks = [s[:, n * HEAD_DIM:(n + 1) * HEAD_DIM] for n in range(GQA_NK // HEAD_DIM)]
        block_max = functools.reduce(jnp.maximum, blocks)
        row_max = jnp.broadcast_to(jnp.max(block_max, axis=-1, keepdims=True), block_max.shape)
        m_new = row_max if m is None else jnp.maximum(m, row_max)
        p = jnp.concatenate([jnp.exp2(b - m_new) for b in blocks], axis=1).astype(BF16)
        pv = jnp.dot(p, vext_ref[keys, :], preferred_element_type=F32)
        if acc is None:
            acc = pv
        else:
            alpha = jnp.exp2(m - m_new)
            acc = jnp.concatenate([acc[:, :HEAD_DIM] * alpha, acc[:, HEAD_DIM:] * alpha], axis=1) + pv
        m = m_new
    o = acc[:, :HEAD_DIM] / acc[:, HEAD_DIM:]
    for g, sl in enumerate(heads):
        o_ref[0, :, sl] = o[g * tq:(g + 1) * tq].astype(BF16)


def _gqa_attention(q3d, kv3d):
    batch, seq_len, _ = q3d.shape
    tq = GQA_TQ
    group_w = GQA_GROUP * HEAD_DIM
    return pl.pallas_call(
        _gqa_kernel,
        grid=(batch, GQA_KV_HEADS, seq_len // tq),
        in_specs=[
            pl.BlockSpec((1, tq, group_w), lambda b, kv, qi: (b, qi, kv)),
            pl.BlockSpec((1, seq_len, HEAD_DIM), lambda b, kv, qi: (b, 0, kv)),
            pl.BlockSpec((1, seq_len, HEAD_DIM), lambda b, kv, qi: (b, 0, GQA_KV_HEADS + kv)),
        ],
        out_specs=pl.BlockSpec((1, tq, group_w), lambda b, kv, qi: (b, qi, kv)),
        out_shape=jax.ShapeDtypeStruct((batch, seq_len, GQA_WIDTH), BF16),
        scratch_shapes=[pltpu.VMEM((seq_len, 2 * HEAD_DIM), BF16)],
        compiler_params=_params("parallel", "parallel", "arbitrary"),
        name="gqa_attention",
    )(q3d, kv3d, kv3d)


MERGE_TM = 512
MERGE_CHUNK = 256


def _merge_kernel(a_ref, b_ref, ga_ref, gb_ref, x_ref, wna_ref, wgq_ref, wmix_ref,
                  lnpost_ref, lnmlp_ref, x1_ref, h2_ref):
    for r in range(0, MERGE_TM, MERGE_CHUNK):
        rows = slice(r, r + MERGE_CHUNK)
        a = jnp.concatenate([a_ref[hh, rows, :] for hh in range(NA_HEADS)], axis=1)
        pa = jnp.dot(a, wna_ref[...], preferred_element_type=F32)
        pb = jnp.dot(b_ref[rows, :], wgq_ref[...], preferred_element_type=F32)
        merged = ga_ref[rows, :].astype(F32) * pa + gb_ref[rows, :].astype(F32) * pb
        y = jnp.dot(merged.astype(BF16), wmix_ref[...], preferred_element_type=F32)
        x1 = x_ref[rows, :] + _rms(y, lnpost_ref[...])
        x1_ref[rows, :] = x1
        h2_ref[rows, :] = _rms(x1, lnmlp_ref[...]).astype(BF16)


def _merge(a, b, gates, x2d, w_na, w_gq, w_mix, ln_post, ln_mlp):
    tokens = x2d.shape[0]
    tm = MERGE_TM
    row = lambda n, c=0: pl.BlockSpec((tm, n), lambda i: (i, c))
    return pl.pallas_call(
        _merge_kernel,
        grid=(tokens // tm,),
        in_specs=[
            pl.BlockSpec((NA_HEADS, tm, HEAD_DIM), lambda i: (0, i, 0)),
            row(GQA_WIDTH), row(D_MODEL, 0), row(D_MODEL, 1), row(D_MODEL),
            _resident((NA_WIDTH, D_MODEL)), _resident((GQA_WIDTH, D_MODEL)),
            _resident((D_MODEL, D_MODEL)), _resident((1, D_MODEL)), _resident((1, D_MODEL)),
        ],
        out_specs=[row(D_MODEL), row(D_MODEL)],
        out_shape=[jax.ShapeDtypeStruct((tokens, D_MODEL), F32),
                   jax.ShapeDtypeStruct((tokens, D_MODEL), BF16)],
        compiler_params=_params("parallel"),
        name="branch_merge",
    )(a, b, gates, gates, x2d, w_na, w_gq, w_mix, ln_post, ln_mlp)


MLP_TM = 1024
MLP_TK = 1024
MLP_CHUNK = 256


def _mlp_kernel(h_ref, w1_ref, w2_ref, f_ref):
    k = pl.program_id(1)

    def step(accumulate):
        for r in range(0, MLP_TM, MLP_CHUNK):
            rows = slice(r, r + MLP_CHUNK)
            u = jnp.maximum(jnp.dot(h_ref[rows, :], w1_ref[...], preferred_element_type=F32), 0.0)
            contrib = jnp.dot((u * u).astype(BF16), w2_ref[...], preferred_element_type=F32)
            if accumulate:
                f_ref[rows, :] += contrib
            else:
                f_ref[rows, :] = contrib

    @pl.when(k == 0)
    def _():
        step(False)

    @pl.when(k > 0)
    def _():
        step(True)


def _mlp(h2, w1, w2):
    tokens = h2.shape[0]
    tm, tk = MLP_TM, MLP_TK
    return pl.pallas_call(
        _mlp_kernel,
        grid=(tokens // tm, D_FF // tk),
        in_specs=[
            pl.BlockSpec((tm, D_MODEL), lambda i, k: (i, 0)),
            pl.BlockSpec((D_MODEL, tk), lambda i, k: (0, k)),
            pl.BlockSpec((tk, D_MODEL), lambda i, k: (k, 0)),
        ],
        out_specs=pl.BlockSpec((tm, D_MODEL), lambda i, k: (i, 0)),
        out_shape=jax.ShapeDtypeStruct((tokens, D_MODEL), F32),
        compiler_params=_params("parallel", "arbitrary"),
        name="mlp",
    )(h2, w1, w2)


PLE_TM = 512
PLE_CHUNK = 256
PLE_TN = 512


def _ple_kernel(x1_ref, f_ref, p_ref, wg_ref, wp_ref, lnpost_ref, lnpre_ref, lnple_ref, o_ref):
    for r in range(0, PLE_TM, PLE_CHUNK):
        rows = slice(r, r + PLE_CHUNK)
        x2 = x1_ref[rows, :] + _rms(f_ref[rows, :], lnpost_ref[...])
        hg = _rms(x2, lnpre_ref[...]).astype(BF16)
        pb = p_ref[rows, :].astype(BF16)
        tiles = []
        for c in range(0, D_MODEL, PLE_TN):
            cols = slice(c, c + PLE_TN)
            gate = jax.nn.sigmoid(jnp.dot(hg, wg_ref[:, cols], preferred_element_type=F32))
            tiles.append(jnp.dot(pb, wp_ref[:, cols], preferred_element_type=F32) * gate)
        e = jnp.concatenate(tiles, axis=1)
        o_ref[rows, :] = x2 + _rms(e, lnple_ref[...])


def _ple(x1, f, p2d, w_gate, w_proj, ln_mlp_post, ln_ple_pre, ln_ple_post):
    tokens = x1.shape[0]
    tm = PLE_TM
    row = lambda n: pl.BlockSpec((tm, n), lambda i: (i, 0))
    return pl.pallas_call(
        _ple_kernel,
        grid=(tokens // tm,),
        in_specs=[
            row(D_MODEL), row(D_MODEL), row(PLE_DIM),
            _resident((D_MODEL, D_MODEL)), _resident((PLE_DIM, D_MODEL)),
            _resident((1, D_MODEL)), _resident((1, D_MODEL)), _resident((1, D_MODEL)),
        ],
        out_specs=row(D_MODEL),
        out_shape=jax.ShapeDtypeStruct((tokens, D_MODEL), F32),
        compiler_params=_params("parallel"),
        name="ple",
    )(x1, f, p2d, w_gate, w_proj, ln_mlp_post, ln_ple_pre, ln_ple_post)


def _rope_tables(seq_len):
    half = HEAD_DIM // 2
    quarter = half // 2
    t = jnp.arange(seq_len)
    freqs = ROPE_THETA ** (-jnp.arange(0, half, 2, dtype=F32) / half)

    def axis_tables(pos):
        ang = pos.astype(F32)[:, None] * freqs[None, :]
        cos, sin = jnp.cos(ang), jnp.sin(ang)
        zero = jnp.zeros_like(sin)
        return (jnp.concatenate([cos, cos], -1), jnp.concatenate([-sin, zero], -1),
                jnp.concatenate([zero, sin], -1))

    row_t = axis_tables(t // GRID_W)
    col_t = axis_tables(t % GRID_W)
    assert quarter * 4 == HEAD_DIM
    return tuple(jnp.concatenate([a, b], -1) for a, b in zip(row_t, col_t))


def _trunk(x, p, weights, bias_tables, rope):
    batch, seq_len, _ = x.shape
    tokens = batch * seq_len
    x2d = x.reshape(tokens, D_MODEL)
    for i, w in enumerate(weights):
        na, gq, kv, gates = _in_projection(x2d, seq_len, w["ln_mix_pre"], w["w_in"], w["q_norm"],
                                           w["k_norm"], *rope)
        a = _neighbourhood_attention(na.reshape(3 * NA_HEADS, batch, seq_len, HEAD_DIM),
                                     bias_tables[i])
        b = _gqa_attention(gq.reshape(batch, seq_len, GQA_WIDTH),
                           kv.reshape(batch, seq_len, 2 * KV_WIDTH))
        x1, h2 = _merge(a.reshape(NA_HEADS, tokens, HEAD_DIM), b.reshape(tokens, GQA_WIDTH), gates, x2d,
                        w["w_na_branch"], w["w_gqa_branch"], w["w_mix_out"], w["ln_mix_post"],
                        w["ln_mlp_pre"])
        f = _mlp(h2, w["w_ff1"], w["w_ff2"])
        x2d = _ple(x1, f, p[i].reshape(tokens, PLE_DIM), w["w_ple_gate"], w["w_ple_proj"],
                   w["ln_mlp_post"], w["ln_ple_pre"], w["ln_ple_post"])
    return x2d.reshape(batch, seq_len, D_MODEL)


_MATRICES = ("w_in", "w_na_branch", "w_gqa_branch", "w_mix_out", "w_ff1", "w_ff2", "w_ple_gate",
             "w_ple_proj")
_VECTORS = ("ln_mix_pre", "q_norm", "k_norm", "ln_mix_post", "ln_mlp_pre", "ln_mlp_post",
            "ln_ple_pre", "ln_ple_post")


def kernel(x_prompt, x_sample, p_prompt, p_sample, ln_mix_pre, w_in, q_norm, k_norm, na_rpb, w_na_branch, w_gqa_branch, w_mix_out, ln_mix_post, ln_mlp_pre, w_ff1, w_ff2, ln_mlp_post, ln_ple_pre, w_ple_gate, w_ple_proj, ln_ple_post):
    named = dict(ln_mix_pre=ln_mix_pre, w_in=w_in, q_norm=q_norm, k_norm=k_norm,
                 w_na_branch=w_na_branch, w_gqa_branch=w_gqa_branch, w_mix_out=w_mix_out,
                 ln_mix_post=ln_mix_post, ln_mlp_pre=ln_mlp_pre, w_ff1=w_ff1, w_ff2=w_ff2,
                 ln_mlp_post=ln_mlp_post, ln_ple_pre=ln_ple_pre, w_ple_gate=w_ple_gate,
                 w_ple_proj=w_ple_proj, ln_ple_post=ln_ple_post)
    depth = w_in.shape[0]
    weights = []
    for i in range(depth):
        layer = {n: named[n][i].astype(BF16) for n in _MATRICES}
        layer.update({n: named[n][i].reshape(1, -1) for n in _VECTORS})
        weights.append(layer)
    bias_tables = [_na_bias_table(na_rpb[i]) for i in range(depth)]
    outs = []
    for x, p in ((x_prompt, p_prompt), (x_sample, p_sample)):
        rope = _rope_tables(x.shape[1])
        outs.append(_trunk(x, p, weights, bias_tables, rope))
    return tuple(outs)
```
